```python
import jax, jax.numpy as jnp
from jax import lax
import numpy as np

D_MODEL = 2048
BATCH = 1
SEQ = 8192
DEPTH = 1

HEAD_DIM = 128
N_HEADS_TOTAL = D_MODEL // HEAD_DIM
NA_HEADS = N_HEADS_TOTAL // 2
GQA_HEADS = N_HEADS_TOTAL - NA_HEADS
GQA_KV_HEADS = max(1, GQA_HEADS // 4)
NA_WIDTH = NA_HEADS * HEAD_DIM
GQA_WIDTH = GQA_HEADS * HEAD_DIM
GQA_KV_WIDTH = GQA_KV_HEADS * HEAD_DIM
MIX_WIDTH = NA_WIDTH + GQA_WIDTH
IN_PROJ_WIDTH = 3 * NA_WIDTH + GQA_WIDTH + 2 * GQA_KV_WIDTH
GRID_W = 64
NA_KH_MAX = 8
NA_KW = 16
Q_BLOCK = 128
ROPE_THETA = 10000.0
D_FF = ((8 * D_MODEL // 3) + 255) // 256 * 256
CONV_W = 3
EPS = 1e-6

kernel_name = "hybrid_natten_gqa_convffn_encoder"


def rms_norm(x, g):
    xf = x.astype(jnp.float32)
    y = xf * lax.rsqrt(jnp.mean(xf * xf, axis=-1, keepdims=True) + EPS)
    return (y * g.astype(jnp.float32)).astype(x.dtype)


def neighbourhood_attention(q, k, v, rpb):
    B, S, H, Dh = q.shape
    rows = S // GRID_W
    kh = min(NA_KH_MAX, rows)
    qg = q.reshape(B, rows, GRID_W, H, Dh)
    kg = k.reshape(B, rows, GRID_W, H, Dh)
    vg = v.reshape(B, rows, GRID_W, H, Dh)
    r = jnp.arange(rows)
    row_start = jnp.clip(r - kh // 2, 0, rows - kh)
    row_idx = row_start[:, None] + jnp.arange(kh)[None, :]
    c = jnp.arange(GRID_W)
    col_start = jnp.clip(c - NA_KW // 2, 0, GRID_W - NA_KW)
    col_in = (c[None, :] >= col_start[:, None]) & (c[None, :] < col_start[:, None] + NA_KW)
    k_rows = kg[:, row_idx]
    v_rows = vg[:, row_idx]
    scores = jnp.einsum('brqhd,brikhd->bhrqik', qg, k_rows).astype(jnp.float32) * (HEAD_DIM ** -0.5)
    dr = row_idx - r[:, None] + (NA_KH_MAX - 1)
    dc = jnp.clip(c[None, :] - c[:, None], -(NA_KW - 1), NA_KW - 1) + (NA_KW - 1)
    bias = rpb.astype(jnp.float32)[:, dr[:, None, :, None], dc[None, :, None, :]]
    bias = jnp.where(col_in[:, None, :], bias, -jnp.inf)
    scores = scores + bias[None]
    p = jax.nn.softmax(scores.reshape(B, H, rows, GRID_W, kh * GRID_W), axis=-1)
    p = p.reshape(B, H, rows, GRID_W, kh, GRID_W).astype(v.dtype)
    out = jnp.einsum('bhrqik,brikhd->brqhd', p, v_rows)
    return out.reshape(B, S, H, Dh)


def axial_rope_tables(S):
    t = jnp.arange(S)
    row = (t // GRID_W).astype(jnp.float32)
    col = (t % GRID_W).astype(jnp.float32)
    axis_dim = HEAD_DIM // 2
    inv_freq = 1.0 / (ROPE_THETA ** (jnp.arange(0, axis_dim, 2, dtype=jnp.float32) / axis_dim))
    ang = jnp.concatenate([row[:, None] * inv_freq[None, :], col[:, None] * inv_freq[None, :]], axis=-1)
    return jnp.cos(ang), jnp.sin(ang)


def apply_rope(x, cos, sin):
    xf = x.astype(jnp.float32).reshape(*x.shape[:-1], HEAD_DIM // 2, 2)
    x0, x1 = xf[..., 0], xf[..., 1]
    cb = cos[None, :, None, :]
    sb = sin[None, :, None, :]
    out = jnp.stack([x0 * cb - x1 * sb, x0 * sb + x1 * cb], axis=-1)
    return out.reshape(x.shape).astype(x.dtype)


def gqa_attention(q, k, v):
    B, S, Hq, Dh = q.shape
    Hkv = k.shape[2]
    g = Hq // Hkv
    nb = S // Q_BLOCK
    qb = q.reshape(B, nb, Q_BLOCK, Hkv, g, Dh).transpose(1, 0, 2, 3, 4, 5)

    def block(qi):
        s = jnp.einsum('bqkgd,bskd->bkgqs', qi, k).astype(jnp.float32) * (HEAD_DIM ** -0.5)
        p = jax.nn.softmax(s, axis=-1).astype(v.dtype)
        return jnp.einsum('bkgqs,bskd->bqkgd', p, v)

    o = lax.map(block, qb)
    return o.transpose(1, 0, 2, 3, 4, 5).reshape(B, S, Hq, Dh)


def depthwise_conv_centred(u, w, b):
    C = u.shape[-1]
    y = lax.conv_general_dilated(
        u, w[:, None, :].astype(u.dtype), window_strides=(1,),
        padding=[(CONV_W // 2, CONV_W // 2)],
        dimension_numbers=('NWC', 'WIO', 'NWC'), feature_group_count=C)
    return y + b.astype(u.dtype)


def hybrid_layer(x, ln1_g, w_in, na_rpb, q_norm_g, k_norm_g, na_out_g, gqa_out_g,
                 w_out, ln2_g, w_up, conv_w, conv_b, w_down, cos, sin):
    B, S, _ = x.shape
    h = rms_norm(x, ln1_g)
    proj = h @ w_in
    o1 = NA_WIDTH
    o2 = o1 + NA_WIDTH
    o3 = o2 + NA_WIDTH
    o4 = o3 + GQA_WIDTH
    o5 = o4 + GQA_KV_WIDTH
    na_q = proj[..., :o1].reshape(B, S, NA_HEADS, HEAD_DIM)
    na_k = proj[..., o1:o2].reshape(B, S, NA_HEADS, HEAD_DIM)
    na_v = proj[..., o2:o3].reshape(B, S, NA_HEADS, HEAD_DIM)
    g_q = proj[..., o3:o4].reshape(B, S, GQA_HEADS, HEAD_DIM)
    g_k = proj[..., o4:o5].reshape(B, S, GQA_KV_HEADS, HEAD_DIM)
    g_v = proj[..., o5:].reshape(B, S, GQA_KV_HEADS, HEAD_DIM)

    na_o = neighbourhood_attention(na_q, na_k, na_v, na_rpb).reshape(B, S, NA_WIDTH)

    g_q = apply_rope(rms_norm(g_q, q_norm_g), cos, sin)
    g_k = apply_rope(rms_norm(g_k, k_norm_g), cos, sin)
    gqa_o = gqa_attention(g_q, g_k, g_v).reshape(B, S, GQA_WIDTH)

    mixed = jnp.concatenate([rms_norm(na_o, na_out_g), rms_norm(gqa_o, gqa_out_g)], axis=-1)
    x = x + mixed @ w_out

    h2 = rms_norm(x, ln2_g)
    u = depthwise_conv_centred(h2 @ w_up, conv_w, conv_b)
    gate = u[..., :D_FF]
    val = u[..., D_FF:]
    x = x + (jax.nn.silu(gate) * val) @ w_down
    return x


def setup_inputs(seed: int = 0) -> dict:
    key = jax.random.key(seed)
    ks = jax.random.split(key, 16)
    f32 = jnp.float32

    def nrm(k, shape, scale):
        return jax.random.normal(k, shape, f32) * scale

    def gain(k, shape):
        return 1.0 + 0.02 * jax.random.normal(k, shape, f32)

    return {
        "x": jax.random.normal(ks[0], (BATCH, SEQ, D_MODEL), f32),
        "ln1_g": gain(ks[1], (DEPTH, D_MODEL)),
        "w_in": nrm(ks[2], (DEPTH, D_MODEL, IN_PROJ_WIDTH), D_MODEL ** -0.5),
        "na_rpb": nrm(ks[3], (DEPTH, NA_HEADS, 2 * NA_KH_MAX - 1, 2 * NA_KW - 1), 0.1),
        "q_norm_g": gain(ks[4], (DEPTH, HEAD_DIM)),
        "k_norm_g": gain(ks[5], (DEPTH, HEAD_DIM)),
        "na_out_g": gain(ks[6], (DEPTH, NA_WIDTH)),
        "gqa_out_g": gain(ks[7], (DEPTH, GQA_WIDTH)),
        "w_out": nrm(ks[8], (DEPTH, MIX_WIDTH, D_MODEL), MIX_WIDTH ** -0.5),
        "ln2_g": gain(ks[9], (DEPTH, D_MODEL)),
        "w_up": nrm(ks[10], (DEPTH, D_MODEL, 2 * D_FF), D_MODEL ** -0.5),
        "conv_w": nrm(ks[11], (DEPTH, CONV_W, 2 * D_FF), CONV_W ** -0.5),
        "conv_b": nrm(ks[12], (DEPTH, 2 * D_FF), 0.02),
        "w_down": nrm(ks[13], (DEPTH, D_FF, D_MODEL), D_FF ** -0.5),
        "final_g": gain(ks[14], (D_MODEL,)),
    }


def reference(x, ln1_g, w_in, na_rpb, q_norm_g, k_norm_g, na_out_g, gqa_out_g,
              w_out, ln2_g, w_up, conv_w, conv_b, w_down, final_g):
    S = x.shape[1]
    cos, sin = axial_rope_tables(S)
    for l in range(DEPTH):
        x = hybrid_layer(x, ln1_g[l], w_in[l], na_rpb[l], q_norm_g[l], k_norm_g[l],
                         na_out_g[l], gqa_out_g[l], w_out[l], ln2_g[l], w_up[l],
                         conv_w[l], conv_b[l], w_down[l], cos, sin)
    return rms_norm(x, final_g)
```

```python
import functools

import numpy as np
import jax
import jax.numpy as jnp
from jax import lax
from jax.experimental import pallas as pl
from jax.experimental.pallas import tpu as pltpu

HEAD_DIM = 128
GRID_W = 64
NA_KH = 8
NA_KW = 16
ROPE_THETA = 10000.0
EPS = 1e-6
CONV_W = 3
MASK_VALUE = -1e30
LANES = 128
BF16_ROWS = 16
F32_ROWS = 8
VMEM_LIMIT = 56 * 1024 * 1024

bf16 = jnp.bfloat16
f32 = jnp.float32


def _rms(x, g):
    ms = jnp.mean(x * x, axis=-1, keepdims=True)
    return x * lax.rsqrt(ms + EPS) * g


def _params(*sem):
    return pltpu.CompilerParams(dimension_semantics=sem, vmem_limit_bytes=VMEM_LIMIT)


def _norm_rope(a, gain, cos, sin_signed):
    an = _rms(a, gain)
    lane = lax.broadcasted_iota(jnp.int32, an.shape, 1)
    partner = jnp.where(lane % 2 == 0, pltpu.roll(an, LANES - 1, 1), pltpu.roll(an, 1, 1))
    return an * cos + partner * sin_signed


def _in_proj_kernel(x_ref, g_ref, w_ref, qg_ref, kg_ref, cos_ref, sin_ref, o_ref, h_ref, *,
                    tn, na_width, gqa_width, kv_width, scale):
    j = pl.program_id(1)

    @pl.when(j == 0)
    def _():
        h_ref[...] = _rms(x_ref[...], g_ref[...]).astype(bf16)

    acc = jnp.dot(h_ref[...], w_ref[...], preferred_element_type=f32)
    naq_end = na_width // tn
    gq_start = 3 * na_width // tn
    gq_end = (3 * na_width + gqa_width) // tn
    heads_per_tile = tn // HEAD_DIM

    @pl.when(j < naq_end)
    def _():
        o_ref[...] = (acc * scale).astype(o_ref.dtype)

    @pl.when((j >= naq_end) & (j < gq_start))
    def _():
        o_ref[...] = acc.astype(o_ref.dtype)

    @pl.when((j >= gq_start) & (j < gq_end))
    def _():
        for a in range(heads_per_tile):
            sl = slice(a * HEAD_DIM, (a + 1) * HEAD_DIM)
            r = _norm_rope(acc[:, sl], qg_ref[...], cos_ref[...], sin_ref[...])
            o_ref[:, sl] = (r * scale).astype(o_ref.dtype)

    @pl.when(j == gq_end)
    def _():
        k_heads = kv_width // HEAD_DIM
        for a in range(k_heads):
            sl = slice(a * HEAD_DIM, (a + 1) * HEAD_DIM)
            r = _norm_rope(acc[:, sl], kg_ref[...], cos_ref[...], sin_ref[...])
            o_ref[:, sl] = r.astype(o_ref.dtype)
        o_ref[:, kv_width:] = acc[:, kv_width:].astype(o_ref.dtype)


def _in_proj(x2, ln_g, w_in, q_g, k_g, cos_t, sin_t, *, na_width, gqa_width, kv_width, scale):
    S, D = x2.shape
    N = w_in.shape[1]
    tm, tn = 1024, 512
    assert S % tm == 0 and N % tn == 0 and na_width % tn == 0 and gqa_width % tn == 0
    assert 2 * kv_width == tn and N == 3 * na_width + gqa_width + 2 * kv_width
    kern = functools.partial(_in_proj_kernel, tn=tn, na_width=na_width, gqa_width=gqa_width,
                             kv_width=kv_width, scale=scale)
    return pl.pallas_call(
        kern,
        grid=(S // tm, N // tn),
        in_specs=[
            pl.BlockSpec((tm, D), lambda i, j: (i, 0)),
            pl.BlockSpec((1, D), lambda i, j: (0, 0)),
            pl.BlockSpec((D, tn), lambda i, j: (0, j)),
            pl.BlockSpec((1, HEAD_DIM), lambda i, j: (0, 0)),
            pl.BlockSpec((1, HEAD_DIM), lambda i, j: (0, 0)),
            pl.BlockSpec((tm, HEAD_DIM), lambda i, j: (i, 0)),
            pl.BlockSpec((tm, HEAD_DIM), lambda i, j: (i, 0)),
        ],
        out_specs=pl.BlockSpec((tm, tn), lambda i, j: (i, j)),
        out_shape=jax.ShapeDtypeStruct((S, N), bf16),
        scratch_shapes=[pltpu.VMEM((tm, D), bf16)],
        compiler_params=_params("parallel", "arbitrary"),
        name="in_proj",
    )(x2, ln_g, w_in, q_g, k_g, cos_t, sin_t)


NA_ROWS_PER_BLOCK = 4


def _na_bias_tables(rpb):
    c = np.arange(GRID_W)
    col_start = np.clip(c - NA_KW // 2, 0, GRID_W - NA_KW)
    col_in = (c[None, :] >= col_start[:, None]) & (c[None, :] < col_start[:, None] + NA_KW)
    dc = np.clip(c[None, :] - c[:, None], -(NA_KW - 1), NA_KW - 1) + (NA_KW - 1)
    onehot = (dc[None] == np.arange(2 * NA_KW - 1)[:, None, None]) & col_in[None]
    col_neg = np.where(col_in, 0.0, MASK_VALUE).astype(np.float32)
    toeplitz = jnp.einsum("hrd,dqk->hrqk", rpb.astype(f32), jnp.asarray(onehot, f32),
                          precision=lax.Precision.HIGHEST) + col_neg
    return toeplitz


def _na_assemble_bias(toeplitz, n_rows):
    H = toeplitz.shape[0]
    R = NA_ROWS_PER_BLOCK
    n_blocks = n_rows // R
    masked = jnp.full((H, GRID_W, GRID_W), MASK_VALUE, f32)
    variants = []
    for b_abs in (0, 1, n_blocks - 1):
        row_blocks = []
        for rq in range(R):
            r = R * b_abs + rq
            rs = min(max(r - NA_KH // 2, 0), n_rows - NA_KH)
            pieces = []
            for u in (-1, 0, 1):
                for rk in range(R):
                    krow = R * (b_abs + u) + rk
                    if 0 <= krow < n_rows and rs <= krow < rs + NA_KH:
                        pieces.append(toeplitz[:, krow - r + NA_KH - 1])
                    else:
                        pieces.append(masked)
            row_blocks.append(jnp.concatenate(pieces, axis=-1))
        variants.append(jnp.concatenate(row_blocks, axis=-2))
    return jnp.stack(variants, axis=0)


def _na_kernel(q_ref, kp_ref, kc_ref, kn_ref, vp_ref, vc_ref, vn_ref, b_ref, o_ref, *, heads):
    for h in range(heads):
        hs = slice(h * HEAD_DIM, (h + 1) * HEAD_DIM)
        q = q_ref[:, hs]
        k = jnp.concatenate([kp_ref[:, hs], kc_ref[:, hs], kn_ref[:, hs]], axis=0)
        v = jnp.concatenate([vp_ref[:, hs], vc_ref[:, hs], vn_ref[:, hs]], axis=0)
        s = lax.dot_general(q, k, (((1,), (1,)), ((), ())), preferred_element_type=f32)
        s = s + b_ref[0, h]
        m = jnp.max(s, axis=-1, keepdims=True)
        p = jnp.exp(s - m)
        l = jnp.sum(p, axis=-1, keepdims=True)
        o = jnp.dot(p.astype(bf16), v, preferred_element_type=f32)
        o_ref[:, hs] = (o / l).astype(o_ref.dtype)


def _na_attention(proj, bias, *, heads, na_width):
    S = proj.shape[0]
    tq = NA_ROWS_PER_BLOCK * GRID_W
    nb = S // tq
    assert S % tq == 0 and nb >= 3 and na_width == heads * HEAD_DIM

    def variant(b):
        return jnp.where(b == 0, 0, jnp.where(b == nb - 1, 2, 1))

    def blk(col, shift):
        return pl.BlockSpec((tq, na_width), lambda b: (jnp.clip(b + shift, 0, nb - 1), col))

    return pl.pallas_call(
        functools.partial(_na_kernel, heads=heads),
        grid=(nb,),
        in_specs=[blk(0, 0), blk(1, -1), blk(1, 0), blk(1, 1), blk(2, -1), blk(2, 0), blk(2, 1),
                  pl.BlockSpec((1, heads, tq, 3 * tq), lambda b: (variant(b), 0, 0, 0))],
        out_specs=pl.BlockSpec((tq, na_width), lambda b: (b, 0)),
        out_shape=jax.ShapeDtypeStruct((S, na_width), bf16),
        compiler_params=_params("arbitrary"),
        name="na_attn",
    )(proj, proj, proj, proj, proj, proj, proj, bias)


def _gqa_kernel(q_ref, k_ref, v_ref, o_ref, m_ref, l_ref, acc_ref, *, group):
    j = pl.program_id(2)

    @pl.when(j == 0)
    def _():
        m_ref[...] = jnp.full(m_ref.shape, MASK_VALUE, f32)
        l_ref[...] = jnp.zeros(l_ref.shape, f32)
        acc_ref[...] = jnp.zeros(acc_ref.shape, f32)

    k = k_ref[...]
    v = v_ref[...]
    for g in range(group):
        hs = slice(g * HEAD_DIM, (g + 1) * HEAD_DIM)
        s = lax.dot_general(q_ref[:, hs], k, (((1,), (1,)), ((), ())), preferred_element_type=f32)
        m_prev = m_ref[g][:, :1]
        l_prev = l_ref[g][:, :1]
        m_new = jnp.maximum(m_prev, jnp.max(s, axis=-1, keepdims=True))
        alpha = jnp.exp(m_prev - m_new)
        p = jnp.exp(s - m_new)
        l_new = alpha * l_prev + jnp.sum(p, axis=-1, keepdims=True)
        acc_ref[g] = acc_ref[g] * alpha + jnp.dot(p.astype(bf16), v, preferred_element_type=f32)
        m_ref[g] = jnp.broadcast_to(m_new, m_ref.shape[1:])
        l_ref[g] = jnp.broadcast_to(l_new, l_ref.shape[1:])

    @pl.when(j == pl.num_programs(2) - 1)
    def _():
        for g in range(group):
            hs = slice(g * HEAD_DIM, (g + 1) * HEAD_DIM)
            o_ref[:, hs] = (acc_ref[g] / l_ref[g][:, :1]).astype(o_ref.dtype)


def _gqa_attention(proj, *, q_col, k_col, v_col, q_heads, kv_heads):
    S = proj.shape[0]
    group = q_heads // kv_heads
    gw = group * HEAD_DIM
    tq, tk = 512, 2048
    assert S % tq == 0 and S % tk == 0 and q_col % gw == 0 and k_col % HEAD_DIM == 0 and v_col % HEAD_DIM == 0
    return pl.pallas_call(
        functools.partial(_gqa_kernel, group=group),
        grid=(kv_heads, S // tq, S // tk),
        in_specs=[
            pl.BlockSpec((tq, gw), lambda h, i, j: (i, q_col // gw + h)),
            pl.BlockSpec((tk, HEAD_DIM), lambda h, i, j: (j, k_col // HEAD_DIM + h)),
            pl.BlockSpec((tk, HEAD_DIM), lambda h, i, j: (j, v_col // HEAD_DIM + h)),
        ],
        out_specs=pl.BlockSpec((tq, gw), lambda h, i, j: (i, h)),
        out_shape=jax.ShapeDtypeStruct((S, q_heads * HEAD_DIM), bf16),
        scratch_shapes=[pltpu.VMEM((group, tq, LANES), f32), pltpu.VMEM((group, tq, LANES), f32),
                        pltpu.VMEM((group, tq, HEAD_DIM), f32)],
        compiler_params=_params("parallel", "parallel", "arbitrary"),
        name="gqa_attn",
    )(proj, proj, proj)


def _out_proj_kernel(na_ref, gq_ref, nag_ref, gqg_ref, w_ref, x_ref, o_ref, *, na_width):
    a = _rms(na_ref[...].astype(f32), nag_ref[...]).astype(bf16)
    b = _rms(gq_ref[...].astype(f32), gqg_ref[...]).astype(bf16)
    o = jnp.dot(a, w_ref[:na_width, :], preferred_element_type=f32)
    o = o + jnp.dot(b, w_ref[na_width:, :], preferred_element_type=f32)
    o_ref[...] = x_ref[...] + o


def _out_proj(na_o, gqa_o, na_g, gqa_g, w_out, x2):
    S, D = x2.shape
    na_width, gqa_width = na_o.shape[1], gqa_o.shape[1]
    tm = 512
    assert S % tm == 0
    return pl.pallas_call(
        functools.partial(_out_proj_kernel, na_width=na_width),
        grid=(S // tm,),
        in_specs=[
            pl.BlockSpec((tm, na_width), lambda i: (i, 0)),
            pl.BlockSpec((tm, gqa_width), lambda i: (i, 0)),
            pl.BlockSpec((1, na_width), lambda i: (0, 0)),
            pl.BlockSpec((1, gqa_width), lambda i: (0, 0)),
            pl.BlockSpec((na_width + gqa_width, D), lambda i: (0, 0)),
            pl.BlockSpec((tm, D), lambda i: (i, 0)),
        ],
        out_specs=pl.BlockSpec((tm, D), lambda i: (i, 0)),
        out_shape=jax.ShapeDtypeStruct((S, D), f32),
        compiler_params=_params("parallel"),
        name="out_proj",
    )(na_o, gqa_o, na_g, gqa_g, w_out, x2)


def _ffn_up_kernel(x_ref, xp_ref, xn_ref, g_ref, wg_ref, wv_ref, cwg_ref, cwv_ref, cbg_ref, cbv_ref,
                   o_ref, h_ref, pg_ref, pv_ref, *, tm):
    i = pl.program_id(0)
    j = pl.program_id(1)
    halo = BF16_ROWS

    @pl.when(j == 0)
    def _():
        g = g_ref[...]
        h_ref[halo:halo + tm, :] = _rms(x_ref[...], g).astype(bf16)
        zeros = jnp.zeros((F32_ROWS, x_ref.shape[1]), f32)
        prev = jnp.where(i > 0, _rms(xp_ref[...], g), 0.0)
        nxt = jnp.where(i < pl.num_programs(0) - 1, _rms(xn_ref[...], g), 0.0)
        h_ref[0:halo, :] = jnp.concatenate([zeros, prev], axis=0).astype(bf16)
        h_ref[halo + tm:, :] = jnp.concatenate([nxt, zeros], axis=0).astype(bf16)

    h = h_ref[...]
    pg_ref[...] = jnp.dot(h, wg_ref[...], preferred_element_type=f32)
    pv_ref[...] = jnp.dot(h, wv_ref[...], preferred_element_type=f32)

    def conv(p_ref, cw_ref, cb_ref):
        cw = cw_ref[...]
        return (cw[0:1] * p_ref[halo - 1:halo - 1 + tm, :] + cw[1:2] * p_ref[halo:halo + tm, :]
                + cw[2:3] * p_ref[halo + 1:halo + 1 + tm, :] + cb_ref[...])

    gate = conv(pg_ref, cwg_ref, cbg_ref)
    val = conv(pv_ref, cwv_ref, cbv_ref)
    o_ref[...] = (gate / (1.0 + jnp.exp(-gate)) * val).astype(o_ref.dtype)


def _ffn_up(x1, ln_g, w_up, conv_w, conv_b, d_ff):
    S, D = x1.shape
    tm, tn = 1024, 512
    assert S % tm == 0 and d_ff % tn == 0 and tm % F32_ROWS == 0
    nj = d_ff // tn
    rows8 = tm // F32_ROWS
    last8 = S // F32_ROWS - 1
    return pl.pallas_call(
        functools.partial(_ffn_up_kernel, tm=tm),
        grid=(S // tm, nj),
        in_specs=[
            pl.BlockSpec((tm, D), lambda i, j: (i, 0)),
            pl.BlockSpec((F32_ROWS, D), lambda i, j: (jnp.maximum(i * rows8 - 1, 0), 0)),
            pl.BlockSpec((F32_ROWS, D), lambda i, j: (jnp.minimum((i + 1) * rows8, last8), 0)),
            pl.BlockSpec((1, D), lambda i, j: (0, 0)),
            pl.BlockSpec((D, tn), lambda i, j: (0, j)),
            pl.BlockSpec((D, tn), lambda i, j: (0, j + nj)),
            pl.BlockSpec((CONV_W, tn), lambda i, j: (0, j)),
            pl.BlockSpec((CONV_W, tn), lambda i, j: (0, j + nj)),
            pl.BlockSpec((1, tn), lambda i, j: (0, j)),
            pl.BlockSpec((1, tn), lambda i, j: (0, j + nj)),
        ],
        out_specs=pl.BlockSpec((tm, tn), lambda i, j: (i, j)),
        out_shape=jax.ShapeDtypeStruct((S, d_ff), bf16),
        scratch_shapes=[pltpu.VMEM((tm + 2 * BF16_ROWS, D), bf16),
                        pltpu.VMEM((tm + 2 * BF16_ROWS, tn), f32),
                        pltpu.VMEM((tm + 2 * BF16_ROWS, tn), f32)],
        compiler_params=_params("parallel", "arbitrary"),
        name="ffn_up",
    )(x1, x1, x1, ln_g, w_up, w_up, conv_w, conv_w, conv_b, conv_b)


def _ffn_down_kernel(a_ref, w_ref, x_ref, g_ref, o_ref, acc_ref, *, final_norm):
    kk = pl.program_id(1)

    @pl.when(kk == 0)
    def _():
        acc_ref[...] = x_ref[...]

    acc_ref[...] += jnp.dot(a_ref[...], w_ref[...], preferred_element_type=f32)

    @pl.when(kk == pl.num_programs(1) - 1)
    def _():
        o_ref[...] = _rms(acc_ref[...], g_ref[...]) if final_norm else acc_ref[...]


def _ffn_down(act, w_down, x1, final_g, *, final_norm):
    S, D = x1.shape
    d_ff = act.shape[1]
    tm, tk = 512, 1408
    assert S % tm == 0 and d_ff % tk == 0 and tk % LANES == 0
    return pl.pallas_call(
        functools.partial(_ffn_down_kernel, final_norm=final_norm),
        grid=(S // tm, d_ff // tk),
        in_specs=[
            pl.BlockSpec((tm, tk), lambda i, k: (i, k)),
            pl.BlockSpec((tk, D), lambda i, k: (k, 0)),
            pl.BlockSpec((tm, D), lambda i, k: (i, 0)),
            pl.BlockSpec((1, D), lambda i, k: (0, 0)),
        ],
        out_specs=pl.BlockSpec((tm, D), lambda i, k: (i, 0)),
        out_shape=jax.ShapeDtypeStruct((S, D), f32),
        scratch_shapes=[pltpu.VMEM((tm, D), f32)],
        compiler_params=_params("parallel", "arbitrary"),
        name="ffn_down",
    )(act, w_down, x1, final_g)


def _rope_tables(S):
    t = jnp.arange(S)
    row = (t // GRID_W).astype(f32)
    col = (t % GRID_W).astype(f32)
    axis_dim = HEAD_DIM // 2
    inv_freq = 1.0 / (ROPE_THETA ** (jnp.arange(0, axis_dim, 2, dtype=f32) / axis_dim))
    ang = jnp.concatenate([row[:, None] * inv_freq[None, :], col[:, None] * inv_freq[None, :]], axis=-1)
    cos, sin = jnp.cos(ang), jnp.sin(ang)
    cos_t = jnp.repeat(cos, 2, axis=-1)
    sin_t = jnp.stack([-sin, sin], axis=-1).reshape(S, HEAD_DIM)
    return cos_t, sin_t


def kernel(x, ln1_g, w_in, na_rpb, q_norm_g, k_norm_g, na_out_g, gqa_out_g, w_out, ln2_g, w_up,
           conv_w, conv_b, w_down, final_g):
    B, S, D = x.shape
    depth = w_in.shape[0]
    assert B == 1 and S % GRID_W == 0
    na_heads = na_rpb.shape[1]
    na_width = na_heads * HEAD_DIM
    gqa_width = gqa_out_g.shape[1]
    q_heads = gqa_width // HEAD_DIM
    kv_width = (w_in.shape[2] - 3 * na_width - gqa_width) // 2
    kv_heads = kv_width // HEAD_DIM
    d_ff = w_down.shape[1]
    scale = HEAD_DIM ** -0.5
    n_rows = S // GRID_W

    cos_t, sin_t = _rope_tables(S)
    xs = x.reshape(S, D)
    for l in range(depth):
        proj = _in_proj(xs, ln1_g[l][None], w_in[l].astype(bf16), q_norm_g[l][None], k_norm_g[l][None],
                        cos_t, sin_t, na_width=na_width, gqa_width=gqa_width, kv_width=kv_width, scale=scale)
        bias = _na_assemble_bias(_na_bias_tables(na_rpb[l]), n_rows)
        na_o = _na_attention(proj, bias, heads=na_heads, na_width=na_width)
        gqa_o = _gqa_attention(proj, q_col=3 * na_width, k_col=3 * na_width + gqa_width,
                               v_col=3 * na_width + gqa_width + kv_width, q_heads=q_heads, kv_heads=kv_heads)
        x1 = _out_proj(na_o, gqa_o, na_out_g[l][None], gqa_out_g[l][None], w_out[l].astype(bf16), xs)
        act = _ffn_up(x1, ln2_g[l][None], w_up[l].astype(bf16), conv_w[l], conv_b[l][None], d_ff)
        xs = _ffn_down(act, w_down[l].astype(bf16), x1, final_g[None], final_norm=(l == depth - 1))
    return xs.reshape(B, S, D)
```

```python
import functools

import numpy as np
import jax
import jax.numpy as jnp
from jax import lax
from jax.experimental import pallas as pl
from jax.experimental.pallas import tpu as pltpu

HEAD_DIM = 128
GRID_W = 64
NA_KH = 8
NA_KW = 16
ROPE_THETA = 10000.0
EPS = 1e-6
CONV_W = 3
MASK_VALUE = -1e30
LANES = 128
BF16_ROWS = 16
F32_ROWS = 8
VMEM_LIMIT = 56 * 1024 * 1024
LOG2E = 1.4426950408889634

bf16 = jnp.bfloat16
f32 = jnp.float32


def _rms(x, g):
    ms = jnp.mean(x * x, axis=-1, keepdims=True)
    return x * lax.rsqrt(ms + EPS) * g


def _params(*sem):
    return pltpu.CompilerParams(dimension_semantics=sem, vmem_limit_bytes=VMEM_LIMIT)


def _norm_rope(a, gain, cos, sin_signed):
    an = _rms(a, gain)
    lane = lax.broadcasted_iota(jnp.int32, an.shape, 1)
    partner = jnp.where(lane % 2 == 0, pltpu.roll(an, LANES - 1, 1), pltpu.roll(an, 1, 1))
    return an * cos + partner * sin_signed


def _in_proj_kernel(x_ref, g_ref, w_ref, qg_ref, kg_ref, cos_ref, sin_ref, o_ref, h_ref, *,
                    tn, na_width, gqa_width, kv_width, scale):
    j = pl.program_id(1)

    @pl.when(j == 0)
    def _():
        h_ref[...] = _rms(x_ref[...], g_ref[...]).astype(bf16)

    acc = jnp.dot(h_ref[...], w_ref[...], preferred_element_type=f32)
    naq_end = na_width // tn
    gq_start = 3 * na_width // tn
    gq_end = (3 * na_width + gqa_width) // tn
    heads_per_tile = tn // HEAD_DIM

    @pl.when(j < naq_end)
    def _():
        o_ref[...] = (acc * scale).astype(o_ref.dtype)

    @pl.when((j >= naq_end) & (j < gq_start))
    def _():
        o_ref[...] = acc.astype(o_ref.dtype)

    @pl.when((j >= gq_start) & (j < gq_end))
    def _():
        for a in range(heads_per_tile):
            sl = slice(a * HEAD_DIM, (a + 1) * HEAD_DIM)
            r = _norm_rope(acc[:, sl], qg_ref[...], cos_ref[...], sin_ref[...])
            o_ref[:, sl] = (r * (scale * LOG2E)).astype(o_ref.dtype)

    @pl.when(j == gq_end)
    def _():
        k_heads = kv_width // HEAD_DIM
        for a in range(k_heads):
            sl = slice(a * HEAD_DIM, (a + 1) * HEAD_DIM)
            r = _norm_rope(acc[:, sl], kg_ref[...], cos_ref[...], sin_ref[...])
            o_ref[:, sl] = r.astype(o_ref.dtype)
        o_ref[:, kv_width:] = acc[:, kv_width:].astype(o_ref.dtype)


def _in_proj(x2, ln_g, w_in, q_g, k_g, cos_t, sin_t, *, na_width, gqa_width, kv_width, scale):
    S, D = x2.shape
    N = w_in.shape[1]
    tm, tn = 1024, 512
    assert S % tm == 0 and N % tn == 0 and na_width % tn == 0 and gqa_width % tn == 0
    assert 2 * kv_width == tn and N == 3 * na_width + gqa_width + 2 * kv_width
    kern = functools.partial(_in_proj_kernel, tn=tn, na_width=na_width, gqa_width=gqa_width,
                             kv_width=kv_width, scale=scale)
    return pl.pallas_call(
        kern,
        grid=(S // tm, N // tn),
        in_specs=[
            pl.BlockSpec((tm, D), lambda i, j: (i, 0)),
            pl.BlockSpec((1, D), lambda i, j: (0, 0)),
            pl.BlockSpec((D, tn), lambda i, j: (0, j)),
            pl.BlockSpec((1, HEAD_DIM), lambda i, j: (0, 0)),
            pl.BlockSpec((1, HEAD_DIM), lambda i, j: (0, 0)),
            pl.BlockSpec((tm, HEAD_DIM), lambda i, j: (i, 0)),
            pl.BlockSpec((tm, HEAD_DIM), lambda i, j: (i, 0)),
        ],
        out_specs=pl.BlockSpec((tm, tn), lambda i, j: (i, j)),
        out_shape=jax.ShapeDtypeStruct((S, N), bf16),
        scratch_shapes=[pltpu.VMEM((tm, D), bf16)],
        compiler_params=_params("parallel", "arbitrary"),
        name="in_proj",
    )(x2, ln_g, w_in, q_g, k_g, cos_t, sin_t)


NA_ROWS_PER_BLOCK = 4


def _na_bias_tables(rpb):
    c = np.arange(GRID_W)
    col_start = np.clip(c - NA_KW // 2, 0, GRID_W - NA_KW)
    col_in = (c[None, :] >= col_start[:, None]) & (c[None, :] < col_start[:, None] + NA_KW)
    dc = np.clip(c[None, :] - c[:, None], -(NA_KW - 1), NA_KW - 1) + (NA_KW - 1)
    onehot = (dc[None] == np.arange(2 * NA_KW - 1)[:, None, None]) & col_in[None]
    col_neg = np.where(col_in, 0.0, MASK_VALUE).astype(np.float32)
    toeplitz = jnp.einsum("hrd,dqk->hrqk", rpb.astype(f32), jnp.asarray(onehot, f32),
                          precision=lax.Precision.HIGHEST) + col_neg
    return toeplitz


def _na_assemble_bias(toeplitz, n_rows):
    H = toeplitz.shape[0]
    R = NA_ROWS_PER_BLOCK
    n_blocks = n_rows // R
    masked = jnp.full((H, GRID_W, GRID_W), MASK_VALUE, f32)
    variants = []
    for b_abs in (0, 1, n_blocks - 1):
        row_blocks = []
        for rq in range(R):
            r = R * b_abs + rq
            rs = min(max(r - NA_KH // 2, 0), n_rows - NA_KH)
            pieces = []
            for u in (-1, 0, 1):
                for rk in range(R):
                    krow = R * (b_abs + u) + rk
                    if 0 <= krow < n_rows and rs <= krow < rs + NA_KH:
                        pieces.append(toeplitz[:, krow - r + NA_KH - 1])
                    else:
                        pieces.append(masked)
            row_blocks.append(jnp.concatenate(pieces, axis=-1))
        variants.append(jnp.concatenate(row_blocks, axis=-2))
    return jnp.stack(variants, axis=0)


def _na_kernel(q_ref, kp_ref, kc_ref, kn_ref, vp_ref, vc_ref, vn_ref, b_ref, o_ref, *, heads):
    for h in range(heads):
        hs = slice(h * HEAD_DIM, (h + 1) * HEAD_DIM)
        q = q_ref[:, hs]
        k = jnp.concatenate([kp_ref[:, hs], kc_ref[:, hs], kn_ref[:, hs]], axis=0)
        v = jnp.concatenate([vp_ref[:, hs], vc_ref[:, hs], vn_ref[:, hs]], axis=0)
        s = lax.dot_general(q, k, (((1,), (1,)), ((), ())), preferred_element_type=f32)
        s = s + b_ref[0, h]
        m = jnp.max(s, axis=-1, keepdims=True)
        p = jnp.exp(s - m)
        l = jnp.sum(p, axis=-1, keepdims=True)
        o = jnp.dot(p.astype(bf16), v, preferred_element_type=f32)
        o_ref[:, hs] = (o / l).astype(o_ref.dtype)


def _na_attention(proj, bias, *, heads, na_width):
    S = proj.shape[0]
    tq = NA_ROWS_PER_BLOCK * GRID_W
    nb = S // tq
    assert S % tq == 0 and nb >= 3 and na_width == heads * HEAD_DIM

    def variant(b):
        return jnp.where(b == 0, 0, jnp.where(b == nb - 1, 2, 1))

    def blk(col, shift):
        return pl.BlockSpec((tq, na_width), lambda b: (jnp.clip(b + shift, 0, nb - 1), col))

    return pl.pallas_call(
        functools.partial(_na_kernel, heads=heads),
        grid=(nb,),
        in_specs=[blk(0, 0), blk(1, -1), blk(1, 0), blk(1, 1), blk(2, -1), blk(2, 0), blk(2, 1),
                  pl.BlockSpec((1, heads, tq, 3 * tq), lambda b: (variant(b), 0, 0, 0))],
        out_specs=pl.BlockSpec((tq, na_width), lambda b: (b, 0)),
        out_shape=jax.ShapeDtypeStruct((S, na_width), bf16),
        compiler_params=_params("arbitrary"),
        name="na_attn",
    )(proj, proj, proj, proj, proj, proj, proj, bias)


def _gqa_kernel(q_ref, k_ref, vt_ref, o_ref, m_ref, l_ref, acc_ref, s0_ref, s1_ref, *, group):
    j = pl.program_id(2)

    @pl.when(j == 0)
    def _():
        m_ref[...] = jnp.full(m_ref.shape, MASK_VALUE, f32)
        l_ref[...] = jnp.zeros(l_ref.shape, f32)
        acc_ref[...] = jnp.zeros(acc_ref.shape, f32)

    k = k_ref[...]
    vt = vt_ref[...]
    s_bufs = (s0_ref, s1_ref)

    def scores(g):
        hs = slice(g * HEAD_DIM, (g + 1) * HEAD_DIM)
        s = lax.dot_general(k, q_ref[:, hs], (((1,), (1,)), ((), ())), preferred_element_type=f32)
        s_bufs[g % 2][...] = s
        return jnp.max(s, axis=0, keepdims=True)

    def accumulate(g, m_cur):
        m_prev = m_ref[g]
        m_new = jnp.maximum(m_prev, m_cur)
        alpha = jnp.exp2(m_prev - m_new)
        p = jnp.exp2(s_bufs[g % 2][...] - m_new)
        l_ref[g] = alpha * l_ref[g] + jnp.sum(p, axis=0, keepdims=True)
        acc_ref[g] = alpha * acc_ref[g] + jnp.dot(vt, p.astype(bf16), preferred_element_type=f32)
        m_ref[g] = m_new

    m_cur = scores(0)
    for g in range(group):
        m_next = scores(g + 1) if g + 1 < group else None
        accumulate(g, m_cur)
        m_cur = m_next

    @pl.when(j == pl.num_programs(2) - 1)
    def _():
        for g in range(group):
            hs = slice(g * HEAD_DIM, (g + 1) * HEAD_DIM)
            o_ref[:, hs] = (acc_ref[g] / l_ref[g]).T.astype(o_ref.dtype)


def _gqa_attention(proj, vt, *, q_col, k_col, q_heads, kv_heads):
    S = proj.shape[0]
    group = q_heads // kv_heads
    gw = group * HEAD_DIM
    tq, tk = 512, 2048
    assert S % tq == 0 and S % tk == 0 and q_col % gw == 0 and k_col % HEAD_DIM == 0
    return pl.pallas_call(
        functools.partial(_gqa_kernel, group=group),
        grid=(kv_heads, S // tq, S // tk),
        in_specs=[
            pl.BlockSpec((tq, gw), lambda h, i, j: (i, q_col // gw + h)),
            pl.BlockSpec((tk, HEAD_DIM), lambda h, i, j: (j, k_col // HEAD_DIM + h)),
            pl.BlockSpec((HEAD_DIM, tk), lambda h, i, j: (h, j)),
        ],
        out_specs=pl.BlockSpec((tq, gw), lambda h, i, j: (i, h)),
        out_shape=jax.ShapeDtypeStruct((S, q_heads * HEAD_DIM), bf16),
        scratch_shapes=[pltpu.VMEM((group, 1, tq), f32), pltpu.VMEM((group, 1, tq), f32),
                        pltpu.VMEM((group, HEAD_DIM, tq), f32),
                        pltpu.VMEM((tk, tq), f32), pltpu.VMEM((tk, tq), f32)],
        compiler_params=_params("parallel", "parallel", "arbitrary"),
        name="gqa_attn",
    )(proj, proj, vt)


def _out_proj_kernel(na_ref, gq_ref, nag_ref, gqg_ref, w_ref, x_ref, o_ref, *, na_width):
    a = _rms(na_ref[...].astype(f32), nag_ref[...]).astype(bf16)
    b = _rms(gq_ref[...].astype(f32), gqg_ref[...]).astype(bf16)
    o = jnp.dot(a, w_ref[:na_width, :], preferred_element_type=f32)
    o = o + jnp.dot(b, w_ref[na_width:, :], preferred_element_type=f32)
    o_ref[...] = x_ref[...] + o


def _out_proj(na_o, gqa_o, na_g, gqa_g, w_out, x2):
    S, D = x2.shape
    na_width, gqa_width = na_o.shape[1], gqa_o.shape[1]
    tm = 512
    assert S % tm == 0
    return pl.pallas_call(
        functools.partial(_out_proj_kernel, na_width=na_width),
        grid=(S // tm,),
        in_specs=[
            pl.BlockSpec((tm, na_width), lambda i: (i, 0)),
            pl.BlockSpec((tm, gqa_width), lambda i: (i, 0)),
            pl.BlockSpec((1, na_width), lambda i: (0, 0)),
            pl.BlockSpec((1, gqa_width), lambda i: (0, 0)),
            pl.BlockSpec((na_width + gqa_width, D), lambda i: (0, 0)),
            pl.BlockSpec((tm, D), lambda i: (i, 0)),
        ],
        out_specs=pl.BlockSpec((tm, D), lambda i: (i, 0)),
        out_shape=jax.ShapeDtypeStruct((S, D), f32),
        compiler_params=_params("parallel"),
        name="out_proj",
    )(na_o, gqa_o, na_g, gqa_g, w_out, x2)


def _ffn_up_kernel(x_ref, xp_ref, xn_ref, g_ref, wg_ref, wv_ref, cwg_ref, cwv_ref, cbg_ref, cbv_ref,
                   o_ref, h_ref, pg_ref, pv_ref, *, tm):
    i = pl.program_id(0)
    j = pl.program_id(1)
    halo = BF16_ROWS

    @pl.when(j == 0)
    def _():
        g = g_ref[...]
        h_ref[halo:halo + tm, :] = _rms(x_ref[...], g).astype(bf16)
        zeros = jnp.zeros((F32_ROWS, x_ref.shape[1]), f32)
        prev = jnp.where(i > 0, _rms(xp_ref[...], g), 0.0)
        nxt = jnp.where(i < pl.num_programs(0) - 1, _rms(xn_ref[...], g), 0.0)
        h_ref[0:halo, :] = jnp.concatenate([zeros, prev], axis=0).astype(bf16)
        h_ref[halo + tm:, :] = jnp.concatenate([nxt, zeros], axis=0).astype(bf16)

    h = h_ref[...]
    pg_ref[...] = jnp.dot(h, wg_ref[...], preferred_element_type=f32)
    pv_ref[...] = jnp.dot(h, wv_ref[...], preferred_element_type=f32)

    def conv(p_ref, cw_ref, cb_ref):
        cw = cw_ref[...]
        return (cw[0:1] * p_ref[halo - 1:halo - 1 + tm, :] + cw[1:2] * p_ref[halo:halo + tm, :]
                + cw[2:3] * p_ref[halo + 1:halo + 1 + tm, :] + cb_ref[...])

    gate = conv(pg_ref, cwg_ref, cbg_ref)
    val = conv(pv_ref, cwv_ref, cbv_ref)
    o_ref[...] = (gate / (1.0 + jnp.exp(-gate)) * val).astype(o_ref.dtype)


def _ffn_up(x1, ln_g, w_up, conv_w, conv_b, d_ff):
    S, D = x1.shape
    tm, tn = 1024, 512
    assert S % tm == 0 and d_ff % tn == 0 and tm % F32_ROWS == 0
    nj = d_ff // tn
    rows8 = tm // F32_ROWS
    last8 = S // F32_ROWS - 1
    return pl.pallas_call(
        functools.partial(_ffn_up_kernel, tm=tm),
        grid=(S // tm, nj),
        in_specs=[
            pl.BlockSpec((tm, D), lambda i, j: (i, 0)),
            pl.BlockSpec((F32_ROWS, D), lambda i, j: (jnp.maximum(i * rows8 - 1, 0), 0)),
            pl.BlockSpec((F32_ROWS, D), lambda i, j: (jnp.minimum((i + 1) * rows8, last8), 0)),
            pl.BlockSpec((1, D), lambda i, j: (0, 0)),
            pl.BlockSpec((D, tn), lambda i, j: (0, j)),
            pl.BlockSpec((D, tn), lambda i, j: (0, j + nj)),
            pl.BlockSpec((CONV_W, tn), lambda i, j: (0, j)),
            pl.BlockSpec((CONV_W, tn), lambda i, j: (0, j + nj)),
            pl.BlockSpec((1, tn), lambda i, j: (0, j)),
            pl.BlockSpec((1, tn), lambda i, j: (0, j + nj)),
        ],
        out_specs=pl.BlockSpec((tm, tn), lambda i, j: (i, j)),
        out_shape=jax.ShapeDtypeStruct((S, d_ff), bf16),
        scratch_shapes=[pltpu.VMEM((tm + 2 * BF16_ROWS, D), bf16),
                        pltpu.VMEM((tm + 2 * BF16_ROWS, tn), f32),
                        pltpu.VMEM((tm + 2 * BF16_ROWS, tn), f32)],
        compiler_params=_params("parallel", "arbitrary"),
        name="ffn_up",
    )(x1, x1, x1, ln_g, w_up, w_up, conv_w, conv_w, conv_b, conv_b)


def _ffn_down_kernel(a_ref, w_ref, x_ref, g_ref, o_ref, acc_ref, *, final_norm):
    kk = pl.program_id(1)

    @pl.when(kk == 0)
    def _():
        acc_ref[...] = x_ref[...]

    acc_ref[...] += jnp.dot(a_ref[...], w_ref[...], preferred_element_type=f32)

    @pl.when(kk == pl.num_programs(1) - 1)
    def _():
        o_ref[...] = _rms(acc_ref[...], g_ref[...]) if final_norm else acc_ref[...]


def _ffn_down(act, w_down, x1, final_g, *, final_norm):
    S, D = x1.shape
    d_ff = act.shape[1]
    tm, tk = 512, 1408
    assert S % tm == 0 and d_ff % tk == 0 and tk % LANES == 0
    return pl.pallas_call(
        functools.partial(_ffn_down_kernel, final_norm=final_norm),
        grid=(S // tm, d_ff // tk),
        in_specs=[
            pl.BlockSpec((tm, tk), lambda i, k: (i, k)),
            pl.BlockSpec((tk, D), lambda i, k: (k, 0)),
            pl.BlockSpec((tm, D), lambda i, k: (i, 0)),
            pl.BlockSpec((1, D), lambda i, k: (0, 0)),
        ],
        out_specs=pl.BlockSpec((tm, D), lambda i, k: (i, 0)),
        out_shape=jax.ShapeDtypeStruct((S, D), f32),
        scratch_shapes=[pltpu.VMEM((tm, D), f32)],
        compiler_params=_params("parallel", "arbitrary"),
        name="ffn_down",
    )(act, w_down, x1, final_g)


def _rope_tables(S):
    t = jnp.arange(S)
    row = (t // GRID_W).astype(f32)
    col = (t % GRID_W).astype(f32)
    axis_dim = HEAD_DIM // 2
    inv_freq = 1.0 / (ROPE_THETA ** (jnp.arange(0, axis_dim, 2, dtype=f32) / axis_dim))
    ang = jnp.concatenate([row[:, None] * inv_freq[None, :], col[:, None] * inv_freq[None, :]], axis=-1)
    cos, sin = jnp.cos(ang), jnp.sin(ang)
    cos_t = jnp.repeat(cos, 2, axis=-1)
    sin_t = jnp.stack([-sin, sin], axis=-1).reshape(S, HEAD_DIM)
    return cos_t, sin_t


def kernel(x, ln1_g, w_in, na_rpb, q_norm_g, k_norm_g, na_out_g, gqa_out_g, w_out, ln2_g, w_up,
           conv_w, conv_b, w_down, final_g):
    B, S, D = x.shape
    depth = w_in.shape[0]
    assert B == 1 and S % GRID_W == 0
    na_heads = na_rpb.shape[1]
    na_width = na_heads * HEAD_DIM
    gqa_width = gqa_out_g.shape[1]
    q_heads = gqa_width // HEAD_DIM
    kv_width = (w_in.shape[2] - 3 * na_width - gqa_width) // 2
    kv_heads = kv_width // HEAD_DIM
    d_ff = w_down.shape[1]
    scale = HEAD_DIM ** -0.5
    n_rows = S // GRID_W

    cos_t, sin_t = _rope_tables(S)
    xs = x.reshape(S, D)
    for l in range(depth):
        proj = _in_proj(xs, ln1_g[l][None], w_in[l].astype(bf16), q_norm_g[l][None], k_norm_g[l][None],
                        cos_t, sin_t, na_width=na_width, gqa_width=gqa_width, kv_width=kv_width, scale=scale)
        bias = _na_assemble_bias(_na_bias_tables(na_rpb[l]), n_rows)
        na_o = _na_attention(proj, bias, heads=na_heads, na_width=na_width)
        vt = proj[:, 3 * na_width + gqa_width + kv_width:].T
        gqa_o = _gqa_attention(proj, vt, q_col=3 * na_width, k_col=3 * na_width + gqa_width,
                               q_heads=q_heads, kv_heads=kv_heads)
        x1 = _out_proj(na_o, gqa_o, na_out_g[l][None], gqa_out_g[l][None], w_out[l].astype(bf16), xs)
        act = _ffn_up(x1, ln2_g[l][None], w_up[l].astype(bf16), conv_w[l], conv_b[l][None], d_ff)
        xs = _ffn_down(act, w_down[l].astype(bf16), x1, final_g[None], final_norm=(l == depth - 1))
    return xs.reshape(B, S, D)
```

```python
import functools

import numpy as np
import jax
import jax.numpy as jnp
from jax import lax
from jax.experimental import pallas as pl
from jax.experimental.pallas import tpu as pltpu

HEAD_DIM = 128
GRID_W = 64
NA_KH = 8
NA_KW = 16
ROPE_THETA = 10000.0
EPS = 1e-6
CONV_W = 3
MASK_VALUE = -1e30
LANES = 128
BF16_ROWS = 16
F32_ROWS = 8
VMEM_LIMIT = 56 * 1024 * 1024
LOG2E = 1.4426950408889634

bf16 = jnp.bfloat16
f32 = jnp.float32


def _rms(x, g):
    ms = jnp.mean(x * x, axis=-1, keepdims=True)
    return x * lax.rsqrt(ms + EPS) * g


def _params(*sem):
    return pltpu.CompilerParams(dimension_semantics=sem, vmem_limit_bytes=VMEM_LIMIT)


def _norm_rope(a, gain, cos, sin_signed):
    an = _rms(a, gain)
    lane = lax.broadcasted_iota(jnp.int32, an.shape, 1)
    partner = jnp.where(lane % 2 == 0, pltpu.roll(an, LANES - 1, 1), pltpu.roll(an, 1, 1))
    return an * cos + partner * sin_signed


def _pipelined_steps(j, nj, produce, consume, bufs):
    @pl.when(j == 0)
    def _():
        produce(bufs[0])

    for par in (0, 1):
        @pl.when((j > 0) & (j < nj) & (j % 2 == par))
        def _():
            produce(bufs[par])
            consume(bufs[1 - par])

    @pl.when(j == nj)
    def _():
        consume(bufs[(nj - 1) % 2])


def _in_proj_kernel(x_ref, g_ref, w_ref, gain_ref, rope_ref, scale_ref, cos_ref, sin_ref, o_ref,
                    h_ref, acc0_ref, acc1_ref, *, nj):
    j = pl.program_id(1)

    @pl.when(j == 0)
    def _():
        h_ref[...] = _rms(x_ref[...], g_ref[...]).astype(bf16)

    def produce(acc_ref):
        acc_ref[...] = jnp.dot(h_ref[...], w_ref[...], preferred_element_type=f32)

    def consume(acc_ref):
        for a in range(o_ref.shape[1] // HEAD_DIM):
            sl = slice(a * HEAD_DIM, (a + 1) * HEAD_DIM)
            acc = acc_ref[:, sl]
            roped = _norm_rope(acc, gain_ref[:, sl], cos_ref[...], sin_ref[...])
            o_ref[:, sl] = (jnp.where(rope_ref[:, sl] > 0.0, roped, acc) * scale_ref[:, sl]).astype(o_ref.dtype)

    _pipelined_steps(j, nj, produce, consume, (acc0_ref, acc1_ref))


def _in_proj(x2, ln_g, w_in, col_gain, col_rope, col_scale, cos_t, sin_t):
    S, D = x2.shape
    N = w_in.shape[1]
    tm, tn = 1024, 512
    assert S % tm == 0 and N % tn == 0
    nj = N // tn

    def prev(j):
        return jnp.maximum(j - 1, 0)

    return pl.pallas_call(
        functools.partial(_in_proj_kernel, nj=nj),
        grid=(S // tm, nj + 1),
        in_specs=[
            pl.BlockSpec((tm, D), lambda i, j: (i, 0)),
            pl.BlockSpec((1, D), lambda i, j: (0, 0)),
            pl.BlockSpec((D, tn), lambda i, j: (0, jnp.minimum(j, nj - 1))),
            pl.BlockSpec((1, tn), lambda i, j: (0, prev(j))),
            pl.BlockSpec((1, tn), lambda i, j: (0, prev(j))),
            pl.BlockSpec((1, tn), lambda i, j: (0, prev(j))),
            pl.BlockSpec((tm, HEAD_DIM), lambda i, j: (i, 0)),
            pl.BlockSpec((tm, HEAD_DIM), lambda i, j: (i, 0)),
        ],
        out_specs=pl.BlockSpec((tm, tn), lambda i, j: (i, prev(j))),
        out_shape=jax.ShapeDtypeStruct((S, N), bf16),
        scratch_shapes=[pltpu.VMEM((tm, D), bf16), pltpu.VMEM((tm, tn), f32), pltpu.VMEM((tm, tn), f32)],
        compiler_params=_params("parallel", "arbitrary"),
        name="in_proj",
    )(x2, ln_g, w_in, col_gain, col_rope, col_scale, cos_t, sin_t)


NA_ROWS_PER_BLOCK = 4


def _na_bias_tables(rpb):
    c = np.arange(GRID_W)
    col_start = np.clip(c - NA_KW // 2, 0, GRID_W - NA_KW)
    col_in = (c[None, :] >= col_start[:, None]) & (c[None, :] < col_start[:, None] + NA_KW)
    dc = np.clip(c[None, :] - c[:, None], -(NA_KW - 1), NA_KW - 1) + (NA_KW - 1)
    onehot = (dc[None] == np.arange(2 * NA_KW - 1)[:, None, None]) & col_in[None]
    col_neg = np.where(col_in, 0.0, MASK_VALUE).astype(np.float32)
    toeplitz = jnp.einsum("hrd,dqk->hrqk", rpb.astype(f32), jnp.asarray(onehot, f32),
                          precision=lax.Precision.HIGHEST) + col_neg
    return toeplitz


def _na_assemble_bias(toeplitz, n_rows):
    H = toeplitz.shape[0]
    R = NA_ROWS_PER_BLOCK
    n_blocks = n_rows // R
    masked = jnp.full((H, GRID_W, GRID_W), MASK_VALUE, f32)
    variants = []
    for b_abs in (0, 1, n_blocks - 1):
        row_blocks = []
        for rq in range(R):
            r = R * b_abs + rq
            rs = min(max(r - NA_KH // 2, 0), n_rows - NA_KH)
            pieces = []
            for u in (-1, 0, 1):
                for rk in range(R):
                    krow = R * (b_abs + u) + rk
                    if 0 <= krow < n_rows and rs <= krow < rs + NA_KH:
                        pieces.append(toeplitz[:, krow - r + NA_KH - 1])
                    else:
                        pieces.append(masked)
            row_blocks.append(jnp.concatenate(pieces, axis=-1))
        variants.append(jnp.concatenate(row_blocks, axis=-2))
    return jnp.stack(variants, axis=0)


def _na_kernel(q_ref, kp_ref, kc_ref, kn_ref, vp_ref, vc_ref, vn_ref, b_ref, o_ref, *, heads):
    for h in range(heads):
        hs = slice(h * HEAD_DIM, (h + 1) * HEAD_DIM)
        q = q_ref[:, hs]
        k = jnp.concatenate([kp_ref[:, hs], kc_ref[:, hs], kn_ref[:, hs]], axis=0)
        v = jnp.concatenate([vp_ref[:, hs], vc_ref[:, hs], vn_ref[:, hs]], axis=0)
        s = lax.dot_general(q, k, (((1,), (1,)), ((), ())), preferred_element_type=f32)
        s = s + b_ref[0, h]
        m = jnp.max(s, axis=-1, keepdims=True)
        p = jnp.exp(s - m)
        l = jnp.sum(p, axis=-1, keepdims=True)
        o = jnp.dot(p.astype(bf16), v, preferred_element_type=f32)
        o_ref[:, hs] = (o / l).astype(o_ref.dtype)


def _na_attention(proj, bias, *, heads, na_width):
    S = proj.shape[0]
    tq = NA_ROWS_PER_BLOCK * GRID_W
    nb = S // tq
    assert S % tq == 0 and nb >= 3 and na_width == heads * HEAD_DIM

    def variant(b):
        return jnp.where(b == 0, 0, jnp.where(b == nb - 1, 2, 1))

    def blk(col, shift):
        return pl.BlockSpec((tq, na_width), lambda b: (jnp.clip(b + shift, 0, nb - 1), col))

    return pl.pallas_call(
        functools.partial(_na_kernel, heads=heads),
        grid=(nb,),
        in_specs=[blk(0, 0), blk(1, -1), blk(1, 0), blk(1, 1), blk(2, -1), blk(2, 0), blk(2, 1),
                  pl.BlockSpec((1, heads, tq, 3 * tq), lambda b: (variant(b), 0, 0, 0))],
        out_specs=pl.BlockSpec((tq, na_width), lambda b: (b, 0)),
        out_shape=jax.ShapeDtypeStruct((S, na_width), bf16),
        compiler_params=_params("arbitrary"),
        name="na_attn",
    )(proj, proj, proj, proj, proj, proj, proj, bias)


def _gqa_kernel(q_ref, k_ref, vt_ref, o_ref, m_ref, l_ref, acc_ref, s0_ref, s1_ref, *, group):
    j = pl.program_id(2)

    @pl.when(j == 0)
    def _():
        m_ref[...] = jnp.full(m_ref.shape, MASK_VALUE, f32)
        l_ref[...] = jnp.zeros(l_ref.shape, f32)
        acc_ref[...] = jnp.zeros(acc_ref.shape, f32)

    k = k_ref[...]
    vt = vt_ref[...]
    s_bufs = (s0_ref, s1_ref)

    def scores(g):
        hs = slice(g * HEAD_DIM, (g + 1) * HEAD_DIM)
        s = lax.dot_general(k, q_ref[:, hs], (((1,), (1,)), ((), ())), preferred_element_type=f32)
        s_bufs[g % 2][...] = s
        return jnp.max(s, axis=0, keepdims=True)

    def accumulate(g, m_cur):
        m_prev = m_ref[g]
        m_new = jnp.maximum(m_prev, m_cur)
        alpha = jnp.exp2(m_prev - m_new)
        p = jnp.exp2(s_bufs[g % 2][...] - m_new)
        l_ref[g] = alpha * l_ref[g] + jnp.sum(p, axis=0, keepdims=True)
        acc_ref[g] = alpha * acc_ref[g] + jnp.dot(vt, p.astype(bf16), preferred_element_type=f32)
        m_ref[g] = m_new

    m_cur = scores(0)
    for g in range(group):
        m_next = scores(g + 1) if g + 1 < group else None
        accumulate(g, m_cur)
        m_cur = m_next

    @pl.when(j == pl.num_programs(2) - 1)
    def _():
        for g in range(group):
            hs = slice(g * HEAD_DIM, (g + 1) * HEAD_DIM)
            o_ref[:, hs] = (acc_ref[g] / l_ref[g]).T.astype(o_ref.dtype)


def _gqa_attention(proj, vt, *, q_col, k_col, q_heads, kv_heads):
    S = proj.shape[0]
    group = q_heads // kv_heads
    gw = group * HEAD_DIM
    tq, tk = 512, 4096
    assert S % tq == 0 and S % tk == 0 and q_col % gw == 0 and k_col % HEAD_DIM == 0
    return pl.pallas_call(
        functools.partial(_gqa_kernel, group=group),
        grid=(kv_heads, S // tq, S // tk),
        in_specs=[
            pl.BlockSpec((tq, gw), lambda h, i, j: (i, q_col // gw + h)),
            pl.BlockSpec((tk, HEAD_DIM), lambda h, i, j: (j, k_col // HEAD_DIM + h)),
            pl.BlockSpec((HEAD_DIM, tk), lambda h, i, j: (h, j)),
        ],
        out_specs=pl.BlockSpec((tq, gw), lambda h, i, j: (i, h)),
        out_shape=jax.ShapeDtypeStruct((S, q_heads * HEAD_DIM), bf16),
        scratch_shapes=[pltpu.VMEM((group, 1, tq), f32), pltpu.VMEM((group, 1, tq), f32),
                        pltpu.VMEM((group, HEAD_DIM, tq), f32),
                        pltpu.VMEM((tk, tq), f32), pltpu.VMEM((tk, tq), f32)],
        compiler_params=_params("parallel", "parallel", "arbitrary"),
        name="gqa_attn",
    )(proj, proj, vt)


def _out_proj_kernel(na_ref, gq_ref, nag_ref, gqg_ref, w_ref, x_ref, o_ref, *, na_width):
    a = _rms(na_ref[...].astype(f32), nag_ref[...]).astype(bf16)
    b = _rms(gq_ref[...].astype(f32), gqg_ref[...]).astype(bf16)
    o = jnp.dot(a, w_ref[:na_width, :], preferred_element_type=f32)
    o = o + jnp.dot(b, w_ref[na_width:, :], preferred_element_type=f32)
    o_ref[...] = x_ref[...] + o


def _out_proj(na_o, gqa_o, na_g, gqa_g, w_out, x2):
    S, D = x2.shape
    na_width, gqa_width = na_o.shape[1], gqa_o.shape[1]
    tm = 512
    assert S % tm == 0
    return pl.pallas_call(
        functools.partial(_out_proj_kernel, na_width=na_width),
        grid=(S // tm,),
        in_specs=[
            pl.BlockSpec((tm, na_width), lambda i: (i, 0)),
            pl.BlockSpec((tm, gqa_width), lambda i: (i, 0)),
            pl.BlockSpec((1, na_width), lambda i: (0, 0)),
            pl.BlockSpec((1, gqa_width), lambda i: (0, 0)),
            pl.BlockSpec((na_width + gqa_width, D), lambda i: (0, 0)),
            pl.BlockSpec((tm, D), lambda i: (i, 0)),
        ],
        out_specs=pl.BlockSpec((tm, D), lambda i: (i, 0)),
        out_shape=jax.ShapeDtypeStruct((S, D), f32),
        compiler_params=_params("parallel"),
        name="out_proj",
    )(na_o, gqa_o, na_g, gqa_g, w_out, x2)


def _ffn_up_kernel(x_ref, xp_ref, xn_ref, g_ref, wg_ref, wv_ref, cwg_ref, cwv_ref, cbg_ref, cbv_ref,
                   o_ref, h_ref, pg_ref, pv_ref, *, tm):
    i = pl.program_id(0)
    j = pl.program_id(1)
    halo = BF16_ROWS

    @pl.when(j == 0)
    def _():
        g = g_ref[...]
        h_ref[halo:halo + tm, :] = _rms(x_ref[...], g).astype(bf16)
        zeros = jnp.zeros((F32_ROWS, x_ref.shape[1]), f32)
        prev = jnp.where(i > 0, _rms(xp_ref[...], g), 0.0)
        nxt = jnp.where(i < pl.num_programs(0) - 1, _rms(xn_ref[...], g), 0.0)
        h_ref[0:halo, :] = jnp.concatenate([zeros, prev], axis=0).astype(bf16)
        h_ref[halo + tm:, :] = jnp.concatenate([nxt, zeros], axis=0).astype(bf16)

    h = h_ref[...]
    pg_ref[...] = jnp.dot(h, wg_ref[...], preferred_element_type=f32)
    pv_ref[...] = jnp.dot(h, wv_ref[...], preferred_element_type=f32)

    def conv(p_ref, cw_ref, cb_ref):
        cw = cw_ref[...]
        p = p_ref[...]
        rows = p.shape[0]
        below = pltpu.roll(p * cw[0:1], 1, 0)
        above = pltpu.roll(p * cw[2:3], rows - 1, 0)
        return (below + p * cw[1:2] + above)[halo:halo + tm] + cb_ref[...]

    gate = conv(pg_ref, cwg_ref, cbg_ref)
    val = conv(pv_ref, cwv_ref, cbv_ref)
    o_ref[...] = (gate / (1.0 + jnp.exp(-gate)) * val).astype(o_ref.dtype)


def _ffn_up(x1, ln_g, w_up, conv_w, conv_b, d_ff):
    S, D = x1.shape
    tm, tn = 1024, 512
    assert S % tm == 0 and d_ff % tn == 0 and tm % F32_ROWS == 0
    nj = d_ff // tn
    rows8 = tm // F32_ROWS
    last8 = S // F32_ROWS - 1
    p_scratch = pltpu.VMEM((tm + 2 * BF16_ROWS, tn), f32)
    return pl.pallas_call(
        functools.partial(_ffn_up_kernel, tm=tm),
        grid=(S // tm, nj),
        in_specs=[
            pl.BlockSpec((tm, D), lambda i, j: (i, 0)),
            pl.BlockSpec((F32_ROWS, D), lambda i, j: (jnp.maximum(i * rows8 - 1, 0), 0)),
            pl.BlockSpec((F32_ROWS, D), lambda i, j: (jnp.minimum((i + 1) * rows8, last8), 0)),
            pl.BlockSpec((1, D), lambda i, j: (0, 0)),
            pl.BlockSpec((D, tn), lambda i, j: (0, j)),
            pl.BlockSpec((D, tn), lambda i, j: (0, j + nj)),
            pl.BlockSpec((CONV_W, tn), lambda i, j: (0, j)),
            pl.BlockSpec((CONV_W, tn), lambda i, j: (0, j + nj)),
            pl.BlockSpec((1, tn), lambda i, j: (0, j)),
            pl.BlockSpec((1, tn), lambda i, j: (0, j + nj)),
        ],
        out_specs=pl.BlockSpec((tm, tn), lambda i, j: (i, j)),
        out_shape=jax.ShapeDtypeStruct((S, d_ff), bf16),
        scratch_shapes=[pltpu.VMEM((tm + 2 * BF16_ROWS, D), bf16), p_scratch, p_scratch],
        compiler_params=_params("parallel", "arbitrary"),
        name="ffn_up",
    )(x1, x1, x1, ln_g, w_up, w_up, conv_w, conv_w, conv_b, conv_b)


def _ffn_down_kernel(a_ref, w_ref, x_ref, g_ref, o_ref, *, final_norm):
    y = x_ref[...] + jnp.dot(a_ref[...], w_ref[...], preferred_element_type=f32)
    o_ref[...] = _rms(y, g_ref[...]) if final_norm else y


def _ffn_down(act, w_down, x1, final_g, *, final_norm):
    S, D = x1.shape
    d_ff = act.shape[1]
    tm = 256
    assert S % tm == 0
    return pl.pallas_call(
        functools.partial(_ffn_down_kernel, final_norm=final_norm),
        grid=(S // tm,),
        in_specs=[
            pl.BlockSpec((tm, d_ff), lambda i: (i, 0)),
            pl.BlockSpec((d_ff, D), lambda i: (0, 0), pipeline_mode=pl.Buffered(1)),
            pl.BlockSpec((tm, D), lambda i: (i, 0)),
            pl.BlockSpec((1, D), lambda i: (0, 0)),
        ],
        out_specs=pl.BlockSpec((tm, D), lambda i: (i, 0)),
        out_shape=jax.ShapeDtypeStruct((S, D), f32),
        compiler_params=_params("arbitrary"),
        name="ffn_down",
    )(act, w_down, x1, final_g)


def _rope_tables(S):
    t = np.arange(S)
    row = (t // GRID_W).astype(np.float32)
    col = (t % GRID_W).astype(np.float32)
    axis_dim = HEAD_DIM // 2
    expo = np.arange(0, axis_dim, 2, dtype=np.float32) / np.float32(axis_dim)
    inv_freq = (np.float32(1.0) / np.power(np.float32(ROPE_THETA), expo)).astype(np.float32)
    ang = np.concatenate([row[:, None] * inv_freq[None, :], col[:, None] * inv_freq[None, :]], axis=-1)
    cos, sin = np.cos(ang).astype(np.float32), np.sin(ang).astype(np.float32)
    cos_t = np.repeat(cos, 2, axis=-1)
    sin_t = np.stack([-sin, sin], axis=-1).reshape(S, HEAD_DIM)
    return jnp.asarray(cos_t), jnp.asarray(sin_t)


def _in_proj_columns(q_g, k_g, *, na_width, gqa_width, kv_width, scale):
    q_heads, k_heads = gqa_width // HEAD_DIM, kv_width // HEAD_DIM
    ones = lambda n: jnp.ones((n,), f32)
    zeros = lambda n: jnp.zeros((n,), f32)
    gain = jnp.concatenate([ones(3 * na_width), jnp.tile(q_g.astype(f32), q_heads),
                            jnp.tile(k_g.astype(f32), k_heads), ones(kv_width)])
    rope = jnp.concatenate([zeros(3 * na_width), ones(gqa_width + kv_width), zeros(kv_width)])
    col_scale = jnp.concatenate([jnp.full((na_width,), scale, f32), ones(2 * na_width),
                                 jnp.full((gqa_width,), scale * LOG2E, f32), ones(2 * kv_width)])
    return gain[None], rope[None], col_scale[None]


def kernel(x, ln1_g, w_in, na_rpb, q_norm_g, k_norm_g, na_out_g, gqa_out_g, w_out, ln2_g, w_up,
           conv_w, conv_b, w_down, final_g):
    B, S, D = x.shape
    depth = w_in.shape[0]
    assert B == 1 and S % GRID_W == 0
    na_heads = na_rpb.shape[1]
    na_width = na_heads * HEAD_DIM
    gqa_width = gqa_out_g.shape[1]
    q_heads = gqa_width // HEAD_DIM
    kv_width = (w_in.shape[2] - 3 * na_width - gqa_width) // 2
    kv_heads = kv_width // HEAD_DIM
    d_ff = w_down.shape[1]
    scale = HEAD_DIM ** -0.5
    n_rows = S // GRID_W

    cos_t, sin_t = _rope_tables(S)
    xs = x.reshape(S, D)
    for l in range(depth):
        col_gain, col_rope, col_scale = _in_proj_columns(
            q_norm_g[l], k_norm_g[l], na_width=na_width, gqa_width=gqa_width, kv_width=kv_width, scale=scale)
        proj = _in_proj(xs, ln1_g[l][None], w_in[l].astype(bf16), col_gain, col_rope, col_scale, cos_t, sin_t)
        bias = _na_assemble_bias(_na_bias_tables(na_rpb[l]), n_rows)
        na_o = _na_attention(proj, bias, heads=na_heads, na_width=na_width)
        vt = proj[:, 3 * na_width + gqa_width + kv_width:].T
        gqa_o = _gqa_attention(proj, vt, q_col=3 * na_width, k_col=3 * na_width + gqa_width,
                               q_heads=q_heads, kv_heads=kv_heads)
        x1 = _out_proj(na_o, gqa_o, na_out_g[l][None], gqa_out_g[l][None], w_out[l].astype(bf16), xs)
        act = _ffn_up(x1, ln2_g[l][None], w_up[l].astype(bf16), conv_w[l], conv_b[l][None], d_ff)
        xs = _ffn_down(act, w_down[l].astype(bf16), x1, final_g[None], final_norm=(l == depth - 1))
    return xs.reshape(B, S, D)
```

```python
import functools

import numpy as np
import jax
import jax.numpy as jnp
from jax import lax
from jax.experimental import pallas as pl
from jax.experimental.pallas import tpu as pltpu

HEAD_DIM = 128
GRID_W = 64
NA_KH = 8
NA_KW = 16
ROPE_THETA = 10000.0
EPS = 1e-6
CONV_W = 3
MASK_VALUE = -1e30
LANES = 128
BF16_ROWS = 16
F32_ROWS = 8
VMEM_LIMIT = 56 * 1024 * 1024
LOG2E = 1.4426950408889634

bf16 = jnp.bfloat16
f32 = jnp.float32


def _rms(x, g):
    ms = jnp.mean(x * x, axis=-1, keepdims=True)
    return x * lax.rsqrt(ms + EPS) * g


def _params(*sem):
    return pltpu.CompilerParams(dimension_semantics=sem, vmem_limit_bytes=VMEM_LIMIT)


def _norm_rope(a, gain, cos, sin_signed):
    an = _rms(a, gain)
    lane = lax.broadcasted_iota(jnp.int32, an.shape, 1)
    partner = jnp.where(lane % 2 == 0, pltpu.roll(an, LANES - 1, 1), pltpu.roll(an, 1, 1))
    return an * cos + partner * sin_signed


def _pipelined_steps(j, nj, produce, consume, bufs):
    @pl.when(j == 0)
    def _():
        produce(bufs[0])

    for par in (0, 1):
        @pl.when((j > 0) & (j < nj) & (j % 2 == par))
        def _():
            produce(bufs[par])
            consume(bufs[1 - par])

    @pl.when(j == nj)
    def _():
        consume(bufs[(nj - 1) % 2])


def _in_proj_kernel(x_ref, g_ref, w_ref, gain_ref, rope_ref, scale_ref, cos_ref, sin_ref, o_ref,
                    h_ref, acc0_ref, acc1_ref, *, nj):
    j = pl.program_id(1)

    @pl.when(j == 0)
    def _():
        h_ref[...] = _rms(x_ref[...], g_ref[...]).astype(bf16)

    def produce(acc_ref):
        acc_ref[...] = jnp.dot(h_ref[...], w_ref[...], preferred_element_type=f32)

    def consume(acc_ref):
        for a in range(o_ref.shape[1] // HEAD_DIM):
            sl = slice(a * HEAD_DIM, (a + 1) * HEAD_DIM)
            acc = acc_ref[:, sl]
            roped = _norm_rope(acc, gain_ref[:, sl], cos_ref[...], sin_ref[...])
            o_ref[:, sl] = (jnp.where(rope_ref[:, sl] > 0.0, roped, acc) * scale_ref[:, sl]).astype(o_ref.dtype)

    _pipelined_steps(j, nj, produce, consume, (acc0_ref, acc1_ref))


def _in_proj(x2, ln_g, w_in, col_gain, col_rope, col_scale, cos_t, sin_t):
    S, D = x2.shape
    N = w_in.shape[1]
    tm, tn = 1024, 512
    assert S % tm == 0 and N % tn == 0
    nj = N // tn

    def prev(j):
        return jnp.maximum(j - 1, 0)

    return pl.pallas_call(
        functools.partial(_in_proj_kernel, nj=nj),
        grid=(S // tm, nj + 1),
        in_specs=[
            pl.BlockSpec((tm, D), lambda i, j: (i, 0)),
            pl.BlockSpec((1, D), lambda i, j: (0, 0)),
            pl.BlockSpec((D, tn), lambda i, j: (0, jnp.minimum(j, nj - 1))),
            pl.BlockSpec((1, tn), lambda i, j: (0, prev(j))),
            pl.BlockSpec((1, tn), lambda i, j: (0, prev(j))),
            pl.BlockSpec((1, tn), lambda i, j: (0, prev(j))),
            pl.BlockSpec((tm, HEAD_DIM), lambda i, j: (i, 0)),
            pl.BlockSpec((tm, HEAD_DIM), lambda i, j: (i, 0)),
        ],
        out_specs=pl.BlockSpec((tm, tn), lambda i, j: (i, prev(j))),
        out_shape=jax.ShapeDtypeStruct((S, N), bf16),
        scratch_shapes=[pltpu.VMEM((tm, D), bf16), pltpu.VMEM((tm, tn), f32), pltpu.VMEM((tm, tn), f32)],
        compiler_params=_params("parallel", "arbitrary"),
        name="in_proj",
    )(x2, ln_g, w_in, col_gain, col_rope, col_scale, cos_t, sin_t)


NA_ROWS_PER_BLOCK = 4


def _na_bias_tables(rpb):
    c = np.arange(GRID_W)
    col_start = np.clip(c - NA_KW // 2, 0, GRID_W - NA_KW)
    col_in = (c[None, :] >= col_start[:, None]) & (c[None, :] < col_start[:, None] + NA_KW)
    dc = np.clip(c[None, :] - c[:, None], -(NA_KW - 1), NA_KW - 1) + (NA_KW - 1)
    onehot = (dc[None] == np.arange(2 * NA_KW - 1)[:, None, None]) & col_in[None]
    col_neg = np.where(col_in, 0.0, MASK_VALUE).astype(np.float32)
    toeplitz = jnp.einsum("hrd,dqk->hrqk", rpb.astype(f32), jnp.asarray(onehot, f32),
                          precision=lax.Precision.HIGHEST) + col_neg
    return toeplitz


def _na_assemble_bias(toeplitz, n_rows):
    H = toeplitz.shape[0]
    R = NA_ROWS_PER_BLOCK
    n_blocks = n_rows // R
    masked = jnp.full((H, GRID_W, GRID_W), MASK_VALUE, f32)
    variants = []
    for b_abs in (0, 1, n_blocks - 1):
        row_blocks = []
        for rq in range(R):
            r = R * b_abs + rq
            rs = min(max(r - NA_KH // 2, 0), n_rows - NA_KH)
            pieces = []
            for u in (-1, 0, 1):
                for rk in range(R):
                    krow = R * (b_abs + u) + rk
                    if 0 <= krow < n_rows and rs <= krow < rs + NA_KH:
                        pieces.append(toeplitz[:, krow - r + NA_KH - 1])
                    else:
                        pieces.append(masked)
            row_blocks.append(jnp.concatenate(pieces, axis=-1))
        variants.append(jnp.concatenate(row_blocks, axis=-2))
    return jnp.stack(variants, axis=0)


def _na_kernel(q_ref, kp_ref, kc_ref, kn_ref, vp_ref, vc_ref, vn_ref, b_ref, o_ref, *, heads):
    for h in range(heads):
        hs = slice(h * HEAD_DIM, (h + 1) * HEAD_DIM)
        q = q_ref[:, hs]
        k = jnp.concatenate([kp_ref[:, hs], kc_ref[:, hs], kn_ref[:, hs]], axis=0)
        v = jnp.concatenate([vp_ref[:, hs], vc_ref[:, hs], vn_ref[:, hs]], axis=0)
        s = lax.dot_general(q, k, (((1,), (1,)), ((), ())), preferred_element_type=f32)
        s = s + b_ref[0, h]
        m = jnp.max(s, axis=-1, keepdims=True)
        p = jnp.exp(s - m)
        l = jnp.sum(p, axis=-1, keepdims=True)
        o = jnp.dot(p.astype(bf16), v, preferred_element_type=f32)
        o_ref[:, hs] = (o / l).astype(o_ref.dtype)


def _na_attention(proj, bias, *, heads, na_width):
    S = proj.shape[0]
    tq = NA_ROWS_PER_BLOCK * GRID_W
    nb = S // tq
    assert S % tq == 0 and nb >= 3 and na_width == heads * HEAD_DIM

    def variant(b):
        return jnp.where(b == 0, 0, jnp.where(b == nb - 1, 2, 1))

    def blk(col, shift):
        return pl.BlockSpec((tq, na_width), lambda b: (jnp.clip(b + shift, 0, nb - 1), col))

    return pl.pallas_call(
        functools.partial(_na_kernel, heads=heads),
        grid=(nb,),
        in_specs=[blk(0, 0), blk(1, -1), blk(1, 0), blk(1, 1), blk(2, -1), blk(2, 0), blk(2, 1),
                  pl.BlockSpec((1, heads, tq, 3 * tq), lambda b: (variant(b), 0, 0, 0))],
        out_specs=pl.BlockSpec((tq, na_width), lambda b: (b, 0)),
        out_shape=jax.ShapeDtypeStruct((S, na_width), bf16),
        compiler_params=_params("arbitrary"),
        name="na_attn",
    )(proj, proj, proj, proj, proj, proj, proj, bias)


SOFTMAX_CHUNK = 64


def _gqa_kernel(q_ref, k_ref, vt_ref, o_ref, m_ref, acc_ref, s0_ref, s1_ref, p0_ref, p1_ref, *, heads, group):
    j = pl.program_id(1)
    vrows = vt_ref.shape[0] // (heads // group)

    @pl.when(j == 0)
    def _():
        m_ref[...] = jnp.full(m_ref.shape, MASK_VALUE, f32)
        acc_ref[...] = jnp.zeros(acc_ref.shape, f32)

    tk, tq = s0_ref.shape
    s_bufs = (s0_ref, s1_ref)
    p_bufs = (p0_ref, p1_ref)
    n_chunks = tk // SOFTMAX_CHUNK

    def scores(g):
        hs = slice(g * HEAD_DIM, (g + 1) * HEAD_DIM)
        ks = slice((g // group) * HEAD_DIM, (g // group + 1) * HEAD_DIM)
        s_ref = s_bufs[g % 2]
        s_ref[...] = lax.dot_general(k_ref[:, ks], q_ref[:, hs], (((1,), (1,)), ((), ())),
                                     preferred_element_type=f32)
        pmax = None
        for c in range(n_chunks):
            chunk = s_ref[c * SOFTMAX_CHUNK:(c + 1) * SOFTMAX_CHUNK, :]
            cm = jnp.max(chunk.reshape(SOFTMAX_CHUNK // F32_ROWS, F32_ROWS, tq), axis=0)
            pmax = cm if pmax is None else jnp.maximum(pmax, cm)
        return jnp.max(pmax, axis=0, keepdims=True)

    def accumulate(g, m_cur):
        s_ref, p_ref = s_bufs[g % 2], p_bufs[g % 2]
        m_prev = m_ref[g]
        m_new = jnp.maximum(m_prev, m_cur)
        alpha = jnp.exp2(m_prev - m_new)
        for c in range(n_chunks):
            rows = slice(c * SOFTMAX_CHUNK, (c + 1) * SOFTMAX_CHUNK)
            p_ref[rows, :] = jnp.exp2(s_ref[rows, :] - m_new).astype(bf16)
        vt = vt_ref[(g // group) * vrows:(g // group + 1) * vrows, :]
        acc_ref[g] = alpha * acc_ref[g] + jnp.dot(vt, p_ref[...], preferred_element_type=f32)
        m_ref[g] = m_new

    m_cur = scores(0)
    for g in range(heads):
        m_next = scores(g + 1) if g + 1 < heads else None
        accumulate(g, m_cur)
        m_cur = m_next

    @pl.when(j == pl.num_programs(1) - 1)
    def _():
        for g in range(heads):
            hs = slice(g * HEAD_DIM, (g + 1) * HEAD_DIM)
            acc = acc_ref[g]
            o_ref[:, hs] = (acc[:HEAD_DIM] / acc[HEAD_DIM:HEAD_DIM + 1]).T.astype(o_ref.dtype)


def _gqa_attention(proj, vt_ext, *, q_col, k_col, q_heads, kv_heads):
    S = proj.shape[0]
    group = q_heads // kv_heads
    qw, kw = q_heads * HEAD_DIM, kv_heads * HEAD_DIM
    vrows = vt_ext.shape[0] // kv_heads
    tq, tk = 512, 4096
    assert S % tq == 0 and S % tk == 0 and q_col % qw == 0 and k_col % kw == 0
    return pl.pallas_call(
        functools.partial(_gqa_kernel, heads=q_heads, group=group),
        grid=(S // tq, S // tk),
        in_specs=[
            pl.BlockSpec((tq, qw), lambda i, j: (i, q_col // qw)),
            pl.BlockSpec((tk, kw), lambda i, j: (j, k_col // kw)),
            pl.BlockSpec((kv_heads * vrows, tk), lambda i, j: (0, j)),
        ],
        out_specs=pl.BlockSpec((tq, qw), lambda i, j: (i, 0)),
        out_shape=jax.ShapeDtypeStruct((S, qw), bf16),
        scratch_shapes=[pltpu.VMEM((q_heads, 1, tq), f32), pltpu.VMEM((q_heads, vrows, tq), f32),
                        pltpu.VMEM((tk, tq), f32), pltpu.VMEM((tk, tq), f32),
                        pltpu.VMEM((tk, tq), bf16), pltpu.VMEM((tk, tq), bf16)],
        compiler_params=_params("parallel", "arbitrary"),
        name="gqa_attn",
    )(proj, proj, vt_ext)


def _out_proj_kernel(na_ref, gq_ref, nag_ref, gqg_ref, w_ref, x_ref, o_ref, *, na_width):
    a = _rms(na_ref[...].astype(f32), nag_ref[...]).astype(bf16)
    b = _rms(gq_ref[...].astype(f32), gqg_ref[...]).astype(bf16)
    o = jnp.dot(a, w_ref[:na_width, :], preferred_element_type=f32)
    o = o + jnp.dot(b, w_ref[na_width:, :], preferred_element_type=f32)
    o_ref[...] = x_ref[...] + o


def _out_proj(na_o, gqa_o, na_g, gqa_g, w_out, x2):
    S, D = x2.shape
    na_width, gqa_width = na_o.shape[1], gqa_o.shape[1]
    tm = 512
    assert S % tm == 0
    return pl.pallas_call(
        functools.partial(_out_proj_kernel, na_width=na_width),
        grid=(S // tm,),
        in_specs=[
            pl.BlockSpec((tm, na_width), lambda i: (i, 0)),
            pl.BlockSpec((tm, gqa_width), lambda i: (i, 0)),
            pl.BlockSpec((1, na_width), lambda i: (0, 0)),
            pl.BlockSpec((1, gqa_width), lambda i: (0, 0)),
            pl.BlockSpec((na_width + gqa_width, D), lambda i: (0, 0)),
            pl.BlockSpec((tm, D), lambda i: (i, 0)),
        ],
        out_specs=pl.BlockSpec((tm, D), lambda i: (i, 0)),
        out_shape=jax.ShapeDtypeStruct((S, D), f32),
        compiler_params=_params("parallel"),
        name="out_proj",
    )(na_o, gqa_o, na_g, gqa_g, w_out, x2)


def _ffn_up_kernel(x_ref, xp_ref, xn_ref, g_ref, wg_ref, wv_ref, cwg_ref, cwv_ref, cbg_ref, cbv_ref,
                   o_ref, h_ref, pg_ref, pv_ref, *, tm):
    i = pl.program_id(0)
    j = pl.program_id(1)
    halo = BF16_ROWS

    @pl.when(j == 0)
    def _():
        g = g_ref[...]
        h_ref[halo:halo + tm, :] = _rms(x_ref[...], g).astype(bf16)
        zeros = jnp.zeros((F32_ROWS, x_ref.shape[1]), f32)
        prev = jnp.where(i > 0, _rms(xp_ref[...], g), 0.0)
        nxt = jnp.where(i < pl.num_programs(0) - 1, _rms(xn_ref[...], g), 0.0)
        h_ref[0:halo, :] = jnp.concatenate([zeros, prev], axis=0).astype(bf16)
        h_ref[halo + tm:, :] = jnp.concatenate([nxt, zeros], axis=0).astype(bf16)

    h = h_ref[...]
    pg_ref[...] = jnp.dot(h, wg_ref[...], preferred_element_type=f32)
    pv_ref[...] = jnp.dot(h, wv_ref[...], preferred_element_type=f32)

    def conv(p_ref, cw_ref, cb_ref):
        cw = cw_ref[...]
        p = p_ref[...]
        rows = p.shape[0]
        below = pltpu.roll(p * cw[0:1], 1, 0)
        above = pltpu.roll(p * cw[2:3], rows - 1, 0)
        return (below + p * cw[1:2] + above)[halo:halo + tm] + cb_ref[...]

    gate = conv(pg_ref, cwg_ref, cbg_ref)
    val = conv(pv_ref, cwv_ref, cbv_ref)
    o_ref[...] = (gate / (1.0 + jnp.exp(-gate)) * val).astype(o_ref.dtype)


def _ffn_up(x1, ln_g, w_up, conv_w, conv_b, d_ff):
    S, D = x1.shape
    tm, tn = 1024, 512
    assert S % tm == 0 and d_ff % tn == 0 and tm % F32_ROWS == 0
    nj = d_ff // tn
    rows8 = tm // F32_ROWS
    last8 = S // F32_ROWS - 1
    p_scratch = pltpu.VMEM((tm + 2 * BF16_ROWS, tn), f32)
    return pl.pallas_call(
        functools.partial(_ffn_up_kernel, tm=tm),
        grid=(S // tm, nj),
        in_specs=[
            pl.BlockSpec((tm, D), lambda i, j: (i, 0)),
            pl.BlockSpec((F32_ROWS, D), lambda i, j: (jnp.maximum(i * rows8 - 1, 0), 0)),
            pl.BlockSpec((F32_ROWS, D), lambda i, j: (jnp.minimum((i + 1) * rows8, last8), 0)),
            pl.BlockSpec((1, D), lambda i, j: (0, 0)),
            pl.BlockSpec((D, tn), lambda i, j: (0, j)),
            pl.BlockSpec((D, tn), lambda i, j: (0, j + nj)),
            pl.BlockSpec((CONV_W, tn), lambda i, j: (0, j)),
            pl.BlockSpec((CONV_W, tn), lambda i, j: (0, j + nj)),
            pl.BlockSpec((1, tn), lambda i, j: (0, j)),
            pl.BlockSpec((1, tn), lambda i, j: (0, j + nj)),
        ],
        out_specs=pl.BlockSpec((tm, tn), lambda i, j: (i, j)),
        out_shape=jax.ShapeDtypeStruct((S, d_ff), bf16),
        scratch_shapes=[pltpu.VMEM((tm + 2 * BF16_ROWS, D), bf16), p_scratch, p_scratch],
        compiler_params=_params("parallel", "arbitrary"),
        name="ffn_up",
    )(x1, x1, x1, ln_g, w_up, w_up, conv_w, conv_w, conv_b, conv_b)


def _ffn_down_kernel(a_ref, w_ref, x_ref, g_ref, o_ref, *, final_norm):
    y = x_ref[...] + jnp.dot(a_ref[...], w_ref[...], preferred_element_type=f32)
    o_ref[...] = _rms(y, g_ref[...]) if final_norm else y


def _ffn_down(act, w_down, x1, final_g, *, final_norm):
    S, D = x1.shape
    d_ff = act.shape[1]
    tm = 256
    assert S % tm == 0
    return pl.pallas_call(
        functools.partial(_ffn_down_kernel, final_norm=final_norm),
        grid=(S // tm,),
        in_specs=[
            pl.BlockSpec((tm, d_ff), lambda i: (i, 0)),
            pl.BlockSpec((d_ff, D), lambda i: (0, 0), pipeline_mode=pl.Buffered(1)),
            pl.BlockSpec((tm, D), lambda i: (i, 0)),
            pl.BlockSpec((1, D), lambda i: (0, 0)),
        ],
        out_specs=pl.BlockSpec((tm, D), lambda i: (i, 0)),
        out_shape=jax.ShapeDtypeStruct((S, D), f32),
        compiler_params=_params("arbitrary"),
        name="ffn_down",
    )(act, w_down, x1, final_g)


def _rope_tables(S):
    t = np.arange(S)
    row = (t // GRID_W).astype(np.float32)
    col = (t % GRID_W).astype(np.float32)
    axis_dim = HEAD_DIM // 2
    expo = np.arange(0, axis_dim, 2, dtype=np.float32) / np.float32(axis_dim)
    inv_freq = (np.float32(1.0) / np.power(np.float32(ROPE_THETA), expo)).astype(np.float32)
    ang = np.concatenate([row[:, None] * inv_freq[None, :], col[:, None] * inv_freq[None, :]], axis=-1)
    cos, sin = np.cos(ang).astype(np.float32), np.sin(ang).astype(np.float32)
    cos_t = np.repeat(cos, 2, axis=-1)
    sin_t = np.stack([-sin, sin], axis=-1).reshape(S, HEAD_DIM)
    return jnp.asarray(cos_t), jnp.asarray(sin_t)


def _in_proj_columns(q_g, k_g, *, na_width, gqa_width, kv_width, scale):
    q_heads, k_heads = gqa_width // HEAD_DIM, kv_width // HEAD_DIM
    ones = lambda n: jnp.ones((n,), f32)
    zeros = lambda n: jnp.zeros((n,), f32)
    gain = jnp.concatenate([ones(3 * na_width), jnp.tile(q_g.astype(f32), q_heads),
                            jnp.tile(k_g.astype(f32), k_heads), ones(kv_width)])
    rope = jnp.concatenate([zeros(3 * na_width), ones(gqa_width + kv_width), zeros(kv_width)])
    col_scale = jnp.concatenate([jnp.full((na_width,), scale, f32), ones(2 * na_width),
                                 jnp.full((gqa_width,), scale * LOG2E, f32), ones(2 * kv_width)])
    return gain[None], rope[None], col_scale[None]


def kernel(x, ln1_g, w_in, na_rpb, q_norm_g, k_norm_g, na_out_g, gqa_out_g, w_out, ln2_g, w_up,
           conv_w, conv_b, w_down, final_g):
    B, S, D = x.shape
    depth = w_in.shape[0]
    assert B == 1 and S % GRID_W == 0
    na_heads = na_rpb.shape[1]
    na_width = na_heads * HEAD_DIM
    gqa_width = gqa_out_g.shape[1]
    q_heads = gqa_width // HEAD_DIM
    kv_width = (w_in.shape[2] - 3 * na_width - gqa_width) // 2
    kv_heads = kv_width // HEAD_DIM
    d_ff = w_down.shape[1]
    scale = HEAD_DIM ** -0.5
    n_rows = S // GRID_W

    cos_t, sin_t = _rope_tables(S)
    xs = x.reshape(S, D)
    for l in range(depth):
        col_gain, col_rope, col_scale = _in_proj_columns(
            q_norm_g[l], k_norm_g[l], na_width=na_width, gqa_width=gqa_width, kv_width=kv_width, scale=scale)
        proj = _in_proj(xs, ln1_g[l][None], w_in[l].astype(bf16), col_gain, col_rope, col_scale, cos_t, sin_t)
        bias = _na_assemble_bias(_na_bias_tables(na_rpb[l]), n_rows)
        na_o = _na_attention(proj, bias, heads=na_heads, na_width=na_width)
        vt = proj[:, 3 * na_width + gqa_width + kv_width:].T.reshape(kv_heads, HEAD_DIM, S)
        ones = jnp.ones((kv_heads, BF16_ROWS, S), bf16)
        vt_ext = jnp.concatenate([vt, ones], axis=1).reshape(kv_heads * (HEAD_DIM + BF16_ROWS), S)
        gqa_o = _gqa_attention(proj, vt_ext, q_col=3 * na_width, k_col=3 * na_width + gqa_width,
                               q_heads=q_heads, kv_heads=kv_heads)
        x1 = _out_proj(na_o, gqa_o, na_out_g[l][None], gqa_out_g[l][None], w_out[l].astype(bf16), xs)
        act = _ffn_up(x1, ln2_g[l][None], w_up[l].astype(bf16), conv_w[l], conv_b[l][None], d_ff)
        xs = _ffn_down(act, w_down[l].astype(bf16), x1, final_g[None], final_norm=(l == depth - 1))
    return xs.reshape(B, S, D)
```

```python
import functools

import numpy as np
import jax
import jax.numpy as jnp
from jax import lax
from jax.experimental import pallas as pl
from jax.experimental.pallas import tpu as pltpu

HEAD_DIM = 128
GRID_W = 64
NA_KH = 8
NA_KW = 16
ROPE_THETA = 10000.0
EPS = 1e-6
CONV_W = 3
MASK_VALUE = -1e30
LANES = 128
BF16_ROWS = 16
F32_ROWS = 8
VMEM_LIMIT = 56 * 1024 * 1024
LOG2E = 1.4426950408889634

bf16 = jnp.bfloat16
f32 = jnp.float32


def _rms(x, g):
    ms = jnp.mean(x * x, axis=-1, keepdims=True)
    return x * lax.rsqrt(ms + EPS) * g


def _params(*sem):
    return pltpu.CompilerParams(dimension_semantics=sem, vmem_limit_bytes=VMEM_LIMIT)


def _norm_rope(a, gain, cos, sin_signed):
    an = _rms(a, gain)
    lane = lax.broadcasted_iota(jnp.int32, an.shape, 1)
    partner = jnp.where(lane % 2 == 0, pltpu.roll(an, LANES - 1, 1), pltpu.roll(an, 1, 1))
    return an * cos + partner * sin_signed


def _pipelined_steps(j, nj, produce, consume, bufs):
    @pl.when(j == 0)
    def _():
        produce(bufs[0])

    for par in (0, 1):
        @pl.when((j > 0) & (j < nj) & (j % 2 == par))
        def _():
            produce(bufs[par])
            consume(bufs[1 - par])

    @pl.when(j == nj)
    def _():
        consume(bufs[(nj - 1) % 2])


def _in_proj_kernel(x_ref, g_ref, w_ref, gain_ref, rope_ref, scale_ref, cos_ref, sin_ref, o_ref,
                    h_ref, acc0_ref, acc1_ref, *, nj):
    j = pl.program_id(1)

    @pl.when(j == 0)
    def _():
        h_ref[...] = _rms(x_ref[...], g_ref[...]).astype(bf16)

    def produce(acc_ref):
        acc_ref[...] = jnp.dot(h_ref[...], w_ref[...].astype(bf16), preferred_element_type=f32)

    def consume(acc_ref):
        for a in range(o_ref.shape[1] // HEAD_DIM):
            sl = slice(a * HEAD_DIM, (a + 1) * HEAD_DIM)
            acc = acc_ref[:, sl]
            roped = _norm_rope(acc, gain_ref[:, sl], cos_ref[...], sin_ref[...])
            o_ref[:, sl] = (jnp.where(rope_ref[:, sl] > 0.0, roped, acc) * scale_ref[:, sl]).astype(o_ref.dtype)

    _pipelined_steps(j, nj, produce, consume, (acc0_ref, acc1_ref))


def _in_proj(x2, ln_g, w_in, col_gain, col_rope, col_scale, cos_t, sin_t):
    S, D = x2.shape
    N = w_in.shape[1]
    tm, tn = 1024, 512
    assert S % tm == 0 and N % tn == 0
    nj = N // tn

    def prev(j):
        return jnp.maximum(j - 1, 0)

    return pl.pallas_call(
        functools.partial(_in_proj_kernel, nj=nj),
        grid=(S // tm, nj + 1),
        in_specs=[
            pl.BlockSpec((tm, D), lambda i, j: (i, 0)),
            pl.BlockSpec((1, D), lambda i, j: (0, 0)),
            pl.BlockSpec((D, tn), lambda i, j: (0, jnp.minimum(j, nj - 1))),
            pl.BlockSpec((1, tn), lambda i, j: (0, prev(j))),
            pl.BlockSpec((1, tn), lambda i, j: (0, prev(j))),
            pl.BlockSpec((1, tn), lambda i, j: (0, prev(j))),
            pl.BlockSpec((tm, HEAD_DIM), lambda i, j: (i, 0)),
            pl.BlockSpec((tm, HEAD_DIM), lambda i, j: (i, 0)),
        ],
        out_specs=pl.BlockSpec((tm, tn), lambda i, j: (i, prev(j))),
        out_shape=jax.ShapeDtypeStruct((S, N), bf16),
        scratch_shapes=[pltpu.VMEM((tm, D), bf16), pltpu.VMEM((tm, tn), f32), pltpu.VMEM((tm, tn), f32)],
        compiler_params=_params("parallel", "arbitrary"),
        name="in_proj",
    )(x2, ln_g, w_in, col_gain, col_rope, col_scale, cos_t, sin_t)


NA_ROWS_PER_BLOCK = 4


def _na_bias_tables(rpb):
    c = np.arange(GRID_W)
    col_start = np.clip(c - NA_KW // 2, 0, GRID_W - NA_KW)
    col_in = (c[None, :] >= col_start[:, None]) & (c[None, :] < col_start[:, None] + NA_KW)
    dc = np.clip(c[None, :] - c[:, None], -(NA_KW - 1), NA_KW - 1) + (NA_KW - 1)
    onehot = (dc[None] == np.arange(2 * NA_KW - 1)[:, None, None]) & col_in[None]
    col_neg = np.where(col_in, 0.0, MASK_VALUE).astype(np.float32)
    toeplitz = jnp.einsum("hrd,dqk->hrqk", rpb.astype(f32) * LOG2E, jnp.asarray(onehot, f32),
                          precision=lax.Precision.HIGHEST) + col_neg
    return toeplitz


def _na_assemble_bias(toeplitz, n_rows):
    H = toeplitz.shape[0]
    R = NA_ROWS_PER_BLOCK
    n_blocks = n_rows // R
    masked = jnp.full((H, GRID_W, GRID_W), MASK_VALUE, f32)
    variants = []
    for b_abs in (0, 1, n_blocks - 1):
        row_blocks = []
        for rq in range(R):
            r = R * b_abs + rq
            rs = min(max(r - NA_KH // 2, 0), n_rows - NA_KH)
            pieces = []
            for u in (-1, 0, 1):
                for rk in range(R):
                    krow = R * (b_abs + u) + rk
                    if 0 <= krow < n_rows and rs <= krow < rs + NA_KH:
                        pieces.append(toeplitz[:, krow - r + NA_KH - 1])
                    else:
                        pieces.append(masked)
            row_blocks.append(jnp.concatenate(pieces, axis=-1))
        variants.append(jnp.concatenate(row_blocks, axis=-2))
    return jnp.stack(variants, axis=0)


def _na_kernel(q_ref, kp_ref, kc_ref, kn_ref, vp_ref, vc_ref, vn_ref, b_ref, o_ref, *, heads):
    for h in range(heads):
        hs = slice(h * HEAD_DIM, (h + 1) * HEAD_DIM)
        q = q_ref[:, hs]
        k = jnp.concatenate([kp_ref[:, hs], kc_ref[:, hs], kn_ref[:, hs]], axis=0)
        v = jnp.concatenate([vp_ref[:, hs], vc_ref[:, hs], vn_ref[:, hs]], axis=0)
        v_ext = jnp.concatenate([v, jnp.ones_like(v)], axis=1)
        s = lax.dot_general(q, k, (((1,), (1,)), ((), ())), preferred_element_type=f32)
        s = s + b_ref[0, h]
        m = jnp.max(s, axis=-1, keepdims=True)
        p = jnp.exp2(s - m)
        o = jnp.dot(p.astype(bf16), v_ext, preferred_element_type=f32)
        o_ref[:, hs] = (o[:, :HEAD_DIM] / o[:, HEAD_DIM:HEAD_DIM + 1]).astype(o_ref.dtype)


def _na_attention(proj, bias, *, heads, na_width):
    S = proj.shape[0]
    tq = NA_ROWS_PER_BLOCK * GRID_W
    nb = S // tq
    assert S % tq == 0 and nb >= 3 and na_width == heads * HEAD_DIM

    def variant(b):
        return jnp.where(b == 0, 0, jnp.where(b == nb - 1, 2, 1))

    def blk(col, shift):
        return pl.BlockSpec((tq, na_width), lambda b: (jnp.clip(b + shift, 0, nb - 1), col))

    return pl.pallas_call(
        functools.partial(_na_kernel, heads=heads),
        grid=(nb,),
        in_specs=[blk(0, 0), blk(1, -1), blk(1, 0), blk(1, 1), blk(2, -1), blk(2, 0), blk(2, 1),
                  pl.BlockSpec((1, heads, tq, 3 * tq), lambda b: (variant(b), 0, 0, 0))],
        out_specs=pl.BlockSpec((tq, na_width), lambda b: (b, 0)),
        out_shape=jax.ShapeDtypeStruct((S, na_width), bf16),
        compiler_params=_params("arbitrary"),
        name="na_attn",
    )(proj, proj, proj, proj, proj, proj, proj, bias)


SOFTMAX_CHUNK = 64


def _gqa_kernel(q_ref, k_ref, vt_ref, o_ref, m_ref, acc_ref, s0_ref, s1_ref, p0_ref, p1_ref, *, heads, group):
    j = pl.program_id(1)
    vrows = vt_ref.shape[0] // (heads // group)

    @pl.when(j == 0)
    def _():
        m_ref[...] = jnp.full(m_ref.shape, MASK_VALUE, f32)
        acc_ref[...] = jnp.zeros(acc_ref.shape, f32)

    tk, tq = s0_ref.shape
    s_bufs = (s0_ref, s1_ref)
    p_bufs = (p0_ref, p1_ref)
    n_chunks = tk // SOFTMAX_CHUNK

    def scores(g):
        hs = slice(g * HEAD_DIM, (g + 1) * HEAD_DIM)
        ks = slice((g // group) * HEAD_DIM, (g // group + 1) * HEAD_DIM)
        s_ref = s_bufs[g % 2]
        s_ref[...] = lax.dot_general(k_ref[:, ks], q_ref[:, hs], (((1,), (1,)), ((), ())),
                                     preferred_element_type=f32)
        pmax = None
        for c in range(n_chunks):
            chunk = s_ref[c * SOFTMAX_CHUNK:(c + 1) * SOFTMAX_CHUNK, :]
            cm = jnp.max(chunk.reshape(SOFTMAX_CHUNK // F32_ROWS, F32_ROWS, tq), axis=0)
            pmax = cm if pmax is None else jnp.maximum(pmax, cm)
        return jnp.max(pmax, axis=0, keepdims=True)

    def accumulate(g, m_cur):
        s_ref, p_ref = s_bufs[g % 2], p_bufs[g % 2]
        m_prev = m_ref[g]
        m_new = jnp.maximum(m_prev, m_cur)
        alpha = jnp.exp2(m_prev - m_new)
        for c in range(n_chunks):
            rows = slice(c * SOFTMAX_CHUNK, (c + 1) * SOFTMAX_CHUNK)
            p_ref[rows, :] = jnp.exp2(s_ref[rows, :] - m_new).astype(bf16)
        vt = vt_ref[(g // group) * vrows:(g // group + 1) * vrows, :]
        acc_ref[g] = alpha * acc_ref[g] + jnp.dot(vt, p_ref[...], preferred_element_type=f32)
        m_ref[g] = m_new

    m_cur = scores(0)
    for g in range(heads):
        m_next = scores(g + 1) if g + 1 < heads else None
        accumulate(g, m_cur)
        m_cur = m_next

    @pl.when(j == pl.num_programs(1) - 1)
    def _():
        for g in range(heads):
            hs = slice(g * HEAD_DIM, (g + 1) * HEAD_DIM)
            acc = acc_ref[g]
            o_ref[:, hs] = (acc[:HEAD_DIM] / acc[HEAD_DIM:HEAD_DIM + 1]).T.astype(o_ref.dtype)


def _gqa_attention(proj, vt_ext, *, q_col, k_col, q_heads, kv_heads):
    S = proj.shape[0]
    group = q_heads // kv_heads
    qw, kw = q_heads * HEAD_DIM, kv_heads * HEAD_DIM
    vrows = vt_ext.shape[0] // kv_heads
    tq, tk = 512, 4096
    assert S % tq == 0 and S % tk == 0 and q_col % qw == 0 and k_col % kw == 0
    return pl.pallas_call(
        functools.partial(_gqa_kernel, heads=q_heads, group=group),
        grid=(S // tq, S // tk),
        in_specs=[
            pl.BlockSpec((tq, qw), lambda i, j: (i, q_col // qw)),
            pl.BlockSpec((tk, kw), lambda i, j: (j, k_col // kw)),
            pl.BlockSpec((kv_heads * vrows, tk), lambda i, j: (0, j)),
        ],
        out_specs=pl.BlockSpec((tq, qw), lambda i, j: (i, 0)),
        out_shape=jax.ShapeDtypeStruct((S, qw), bf16),
        scratch_shapes=[pltpu.VMEM((q_heads, 1, tq), f32), pltpu.VMEM((q_heads, vrows, tq), f32),
                        pltpu.VMEM((tk, tq), f32), pltpu.VMEM((tk, tq), f32),
                        pltpu.VMEM((tk, tq), bf16), pltpu.VMEM((tk, tq), bf16)],
        compiler_params=_params("parallel", "arbitrary"),
        name="gqa_attn",
    )(proj, proj, vt_ext)


def _out_proj_kernel(na_ref, gq_ref, nag_ref, gqg_ref, w_ref, x_ref, o_ref, wb_ref, *, na_width):
    @pl.when(pl.program_id(0) == 0)
    def _():
        wb_ref[...] = w_ref[...].astype(bf16)

    a = _rms(na_ref[...].astype(f32), nag_ref[...]).astype(bf16)
    b = _rms(gq_ref[...].astype(f32), gqg_ref[...]).astype(bf16)
    o = jnp.dot(a, wb_ref[:na_width, :], preferred_element_type=f32)
    o = o + jnp.dot(b, wb_ref[na_width:, :], preferred_element_type=f32)
    o_ref[...] = x_ref[...] + o


def _out_proj(na_o, gqa_o, na_g, gqa_g, w_out, x2):
    S, D = x2.shape
    na_width, gqa_width = na_o.shape[1], gqa_o.shape[1]
    tm = 512
    assert S % tm == 0
    return pl.pallas_call(
        functools.partial(_out_proj_kernel, na_width=na_width),
        grid=(S // tm,),
        in_specs=[
            pl.BlockSpec((tm, na_width), lambda i: (i, 0)),
            pl.BlockSpec((tm, gqa_width), lambda i: (i, 0)),
            pl.BlockSpec((1, na_width), lambda i: (0, 0)),
            pl.BlockSpec((1, gqa_width), lambda i: (0, 0)),
            pl.BlockSpec((na_width + gqa_width, D), lambda i: (0, 0), pipeline_mode=pl.Buffered(1)),
            pl.BlockSpec((tm, D), lambda i: (i, 0)),
        ],
        out_specs=pl.BlockSpec((tm, D), lambda i: (i, 0)),
        out_shape=jax.ShapeDtypeStruct((S, D), f32),
        scratch_shapes=[pltpu.VMEM((na_width + gqa_width, D), bf16)],
        compiler_params=_params("arbitrary"),
        name="out_proj",
    )(na_o, gqa_o, na_g, gqa_g, w_out, x2)


def _ffn_up_kernel(x_ref, xp_ref, xn_ref, g_ref, wg_ref, wv_ref, cwg_ref, cwv_ref, cbg_ref, cbv_ref,
                   o_ref, h_ref, pg_ref, pv_ref, *, tm):
    i = pl.program_id(0)
    j = pl.program_id(1)
    halo = BF16_ROWS

    @pl.when(j == 0)
    def _():
        g = g_ref[...]
        h_ref[halo:halo + tm, :] = _rms(x_ref[...], g).astype(bf16)
        zeros = jnp.zeros((F32_ROWS, x_ref.shape[1]), f32)
        prev = jnp.where(i > 0, _rms(xp_ref[...], g), 0.0)
        nxt = jnp.where(i < pl.num_programs(0) - 1, _rms(xn_ref[...], g), 0.0)
        h_ref[0:halo, :] = jnp.concatenate([zeros, prev], axis=0).astype(bf16)
        h_ref[halo + tm:, :] = jnp.concatenate([nxt, zeros], axis=0).astype(bf16)

    h = h_ref[...]
    pg_ref[...] = jnp.dot(h, wg_ref[...].astype(bf16), preferred_element_type=f32)
    pv_ref[...] = jnp.dot(h, wv_ref[...].astype(bf16), preferred_element_type=f32)

    def conv(p_ref, cw_ref, cb_ref):
        cw = cw_ref[...]
        p = p_ref[...]
        rows = p.shape[0]
        below = pltpu.roll(p * cw[0:1], 1, 0)
        above = pltpu.roll(p * cw[2:3], rows - 1, 0)
        return (below + p * cw[1:2] + above)[halo:halo + tm] + cb_ref[...]

    gate = conv(pg_ref, cwg_ref, cbg_ref)
    val = conv(pv_ref, cwv_ref, cbv_ref)
    o_ref[...] = (gate / (1.0 + jnp.exp(-gate)) * val).astype(o_ref.dtype)


def _ffn_up(x1, ln_g, w_up, conv_w, conv_b, d_ff):
    S, D = x1.shape
    tm, tn = 1024, 512
    assert S % tm == 0 and d_ff % tn == 0 and tm % F32_ROWS == 0
    nj = d_ff // tn
    rows8 = tm // F32_ROWS
    last8 = S // F32_ROWS - 1
    p_scratch = pltpu.VMEM((tm + 2 * BF16_ROWS, tn), f32)
    return pl.pallas_call(
        functools.partial(_ffn_up_kernel, tm=tm),
        grid=(S // tm, nj),
        in_specs=[
            pl.BlockSpec((tm, D), lambda i, j: (i, 0)),
            pl.BlockSpec((F32_ROWS, D), lambda i, j: (jnp.maximum(i * rows8 - 1, 0), 0)),
            pl.BlockSpec((F32_ROWS, D), lambda i, j: (jnp.minimum((i + 1) * rows8, last8), 0)),
            pl.BlockSpec((1, D), lambda i, j: (0, 0)),
            pl.BlockSpec((D, tn), lambda i, j: (0, j)),
            pl.BlockSpec((D, tn), lambda i, j: (0, j + nj)),
            pl.BlockSpec((CONV_W, tn), lambda i, j: (0, j)),
            pl.BlockSpec((CONV_W, tn), lambda i, j: (0, j + nj)),
            pl.BlockSpec((1, tn), lambda i, j: (0, j)),
            pl.BlockSpec((1, tn), lambda i, j: (0, j + nj)),
        ],
        out_specs=pl.BlockSpec((tm, tn), lambda i, j: (i, j)),
        out_shape=jax.ShapeDtypeStruct((S, d_ff), bf16),
        scratch_shapes=[pltpu.VMEM((tm + 2 * BF16_ROWS, D), bf16), p_scratch, p_scratch],
        compiler_params=_params("parallel", "arbitrary"),
        name="ffn_up",
    )(x1, x1, x1, ln_g, w_up, w_up, conv_w, conv_w, conv_b, conv_b)


def _ffn_down_kernel(a_ref, w_ref, x_ref, g_ref, o_ref, *, final_norm):
    y = x_ref[...] + jnp.dot(a_ref[...], w_ref[...], preferred_element_type=f32)
    o_ref[...] = _rms(y, g_ref[...]) if final_norm else y


def _ffn_down(act, w_down, x1, final_g, *, final_norm):
    S, D = x1.shape
    d_ff = act.shape[1]
    tm = 256
    assert S % tm == 0
    return pl.pallas_call(
        functools.partial(_ffn_down_kernel, final_norm=final_norm),
        grid=(S // tm,),
        in_specs=[
            pl.BlockSpec((tm, d_ff), lambda i: (i, 0)),
            pl.BlockSpec((d_ff, D), lambda i: (0, 0), pipeline_mode=pl.Buffered(1)),
            pl.BlockSpec((tm, D), lambda i: (i, 0)),
            pl.BlockSpec((1, D), lambda i: (0, 0)),
        ],
        out_specs=pl.BlockSpec((tm, D), lambda i: (i, 0)),
        out_shape=jax.ShapeDtypeStruct((S, D), f32),
        compiler_params=_params("arbitrary"),
        name="ffn_down",
    )(act, w_down, x1, final_g)


def _rope_tables(S):
    t = np.arange(S)
    row = (t // GRID_W).astype(np.float32)
    col = (t % GRID_W).astype(np.float32)
    axis_dim = HEAD_DIM // 2
    expo = np.arange(0, axis_dim, 2, dtype=np.float32) / np.float32(axis_dim)
    inv_freq = (np.float32(1.0) / np.power(np.float32(ROPE_THETA), expo)).astype(np.float32)
    ang = np.concatenate([row[:, None] * inv_freq[None, :], col[:, None] * inv_freq[None, :]], axis=-1)
    cos, sin = np.cos(ang).astype(np.float32), np.sin(ang).astype(np.float32)
    cos_t = np.repeat(cos, 2, axis=-1)
    sin_t = np.stack([-sin, sin], axis=-1).reshape(S, HEAD_DIM)
    return jnp.asarray(cos_t), jnp.asarray(sin_t)


def _in_proj_columns(q_g, k_g, *, na_width, gqa_width, kv_width, scale):
    q_heads, k_heads = gqa_width // HEAD_DIM, kv_width // HEAD_DIM
    ones = lambda n: jnp.ones((n,), f32)
    zeros = lambda n: jnp.zeros((n,), f32)
    gain = jnp.concatenate([ones(3 * na_width), jnp.tile(q_g.astype(f32), q_heads),
                            jnp.tile(k_g.astype(f32), k_heads), ones(kv_width)])
    rope = jnp.concatenate([zeros(3 * na_width), ones(gqa_width + kv_width), zeros(kv_width)])
    col_scale = jnp.concatenate([jnp.full((na_width,), scale * LOG2E, f32), ones(2 * na_width),
                                 jnp.full((gqa_width,), scale * LOG2E, f32), ones(2 * kv_width)])
    return gain[None], rope[None], col_scale[None]


def kernel(x, ln1_g, w_in, na_rpb, q_norm_g, k_norm_g, na_out_g, gqa_out_g, w_out, ln2_g, w_up,
           conv_w, conv_b, w_down, final_g):
    B, S, D = x.shape
    depth = w_in.shape[0]
    assert B == 1 and S % GRID_W == 0
    na_heads = na_rpb.shape[1]
    na_width = na_heads * HEAD_DIM
    gqa_width = gqa_out_g.shape[1]
    q_heads = gqa_width // HEAD_DIM
    kv_width = (w_in.shape[2] - 3 * na_width - gqa_width) // 2
    kv_heads = kv_width // HEAD_DIM
    d_ff = w_down.shape[1]
    scale = HEAD_DIM ** -0.5
    n_rows = S // GRID_W

    cos_t, sin_t = _rope_tables(S)
    xs = x.reshape(S, D)
    for l in range(depth):
        col_gain, col_rope, col_scale = _in_proj_columns(
            q_norm_g[l], k_norm_g[l], na_width=na_width, gqa_width=gqa_width, kv_width=kv_width, scale=scale)
        proj = _in_proj(xs, ln1_g[l][None], w_in[l], col_gain, col_rope, col_scale, cos_t, sin_t)
        bias = _na_assemble_bias(_na_bias_tables(na_rpb[l]), n_rows)
        na_o = _na_attention(proj, bias, heads=na_heads, na_width=na_width)
        vt = proj[:, 3 * na_width + gqa_width + kv_width:].T.reshape(kv_heads, HEAD_DIM, S)
        ones = jnp.ones((kv_heads, BF16_ROWS, S), bf16)
        vt_ext = jnp.concatenate([vt, ones], axis=1).reshape(kv_heads * (HEAD_DIM + BF16_ROWS), S)
        gqa_o = _gqa_attention(proj, vt_ext, q_col=3 * na_width, k_col=3 * na_width + gqa_width,
                               q_heads=q_heads, kv_heads=kv_heads)
        x1 = _out_proj(na_o, gqa_o, na_out_g[l][None], gqa_out_g[l][None], w_out[l], xs)
        act = _ffn_up(x1, ln2_g[l][None], w_up[l], conv_w[l], conv_b[l][None], d_ff)
        xs = _ffn_down(act, w_down[l].astype(bf16), x1, final_g[None], final_norm=(l == depth - 1))
    return xs.reshape(B, S, D)
```

```python
import functools

import numpy as np
import jax
import jax.numpy as jnp
from jax import lax
from jax.experimental import pallas as pl
from jax.experimental.pallas import tpu as pltpu

HEAD_DIM = 128
GRID_W = 64
NA_KH = 8
NA_KW = 16
ROPE_THETA = 10000.0
EPS = 1e-6
CONV_W = 3
MASK_VALUE = -1e30
LANES = 128
BF16_ROWS = 16
F32_ROWS = 8
VMEM_LIMIT = 56 * 1024 * 1024
LOG2E = 1.4426950408889634

bf16 = jnp.bfloat16
f32 = jnp.float32


def _rms(x, g):
    ms = jnp.mean(x * x, axis=-1, keepdims=True)
    return x * lax.rsqrt(ms + EPS) * g


def _params(*sem):
    return pltpu.CompilerParams(dimension_semantics=sem, vmem_limit_bytes=VMEM_LIMIT)


def _norm_rope(a, gain, cos, sin_signed):
    an = _rms(a, gain)
    lane = lax.broadcasted_iota(jnp.int32, an.shape, 1)
    partner = jnp.where(lane % 2 == 0, pltpu.roll(an, LANES - 1, 1), pltpu.roll(an, 1, 1))
    return an * cos + partner * sin_signed


def _pipelined_steps(j, nj, produce, consume, bufs):
    @pl.when(j == 0)
    def _():
        produce(bufs[0])

    for par in (0, 1):
        @pl.when((j > 0) & (j < nj) & (j % 2 == par))
        def _():
            produce(bufs[par])
            consume(bufs[1 - par])

    @pl.when(j == nj)
    def _():
        consume(bufs[(nj - 1) % 2])


def _in_proj_kernel(x_ref, g_ref, w_ref, gain_ref, rope_ref, scale_ref, cos_ref, sin_ref, o_ref,
                    h_ref, acc0_ref, acc1_ref, *, nj):
    j = pl.program_id(1)

    @pl.when(j == 0)
    def _():
        h_ref[...] = _rms(x_ref[...], g_ref[...]).astype(bf16)

    def produce(acc_ref):
        acc_ref[...] = jnp.dot(h_ref[...], w_ref[...].astype(bf16), preferred_element_type=f32)

    def consume(acc_ref):
        for a in range(o_ref.shape[1] // HEAD_DIM):
            sl = slice(a * HEAD_DIM, (a + 1) * HEAD_DIM)
            acc = acc_ref[:, sl]
            roped = _norm_rope(acc, gain_ref[:, sl], cos_ref[...], sin_ref[...])
            o_ref[:, sl] = (jnp.where(rope_ref[:, sl] > 0.0, roped, acc) * scale_ref[:, sl]).astype(o_ref.dtype)

    _pipelined_steps(j, nj, produce, consume, (acc0_ref, acc1_ref))


def _in_proj(x2, ln_g, w_in, col_gain, col_rope, col_scale, cos_t, sin_t):
    S, D = x2.shape
    N = w_in.shape[1]
    tm, tn = 1024, 512
    assert S % tm == 0 and N % tn == 0
    nj = N // tn

    def prev(j):
        return jnp.maximum(j - 1, 0)

    return pl.pallas_call(
        functools.partial(_in_proj_kernel, nj=nj),
        grid=(S // tm, nj + 1),
        in_specs=[
            pl.BlockSpec((tm, D), lambda i, j: (i, 0)),
            pl.BlockSpec((1, D), lambda i, j: (0, 0)),
            pl.BlockSpec((D, tn), lambda i, j: (0, jnp.minimum(j, nj - 1))),
            pl.BlockSpec((1, tn), lambda i, j: (0, prev(j))),
            pl.BlockSpec((1, tn), lambda i, j: (0, prev(j))),
            pl.BlockSpec((1, tn), lambda i, j: (0, prev(j))),
            pl.BlockSpec((tm, HEAD_DIM), lambda i, j: (i, 0)),
            pl.BlockSpec((tm, HEAD_DIM), lambda i, j: (i, 0)),
        ],
        out_specs=pl.BlockSpec((tm, tn), lambda i, j: (i, prev(j))),
        out_shape=jax.ShapeDtypeStruct((S, N), bf16),
        scratch_shapes=[pltpu.VMEM((tm, D), bf16), pltpu.VMEM((tm, tn), f32), pltpu.VMEM((tm, tn), f32)],
        compiler_params=_params("parallel", "arbitrary"),
        name="in_proj",
    )(x2, ln_g, w_in, col_gain, col_rope, col_scale, cos_t, sin_t)


NA_ROWS_PER_BLOCK = 4


def _na_bias_tables(rpb, n_rows):
    H, n_dr, n_dc = rpb.shape
    R = NA_ROWS_PER_BLOCK
    n_blocks = n_rows // R
    c = np.arange(GRID_W)
    col_start = np.clip(c - NA_KW // 2, 0, GRID_W - NA_KW)
    col_in = (c[None, :] >= col_start[:, None]) & (c[None, :] < col_start[:, None] + NA_KW)
    dc = np.clip(c[None, :] - c[:, None], -(NA_KW - 1), NA_KW - 1) + (NA_KW - 1)
    onehot = ((dc[None] == np.arange(n_dc)[:, None, None]) & col_in[None]).astype(np.float32)

    n_feat = 2 * n_dc + 3
    rhs = np.zeros((n_feat, GRID_W, 2 * GRID_W), np.float32)
    rhs[0:n_dc, :, :GRID_W] = onehot
    rhs[n_dc:2 * n_dc, :, GRID_W:] = onehot
    rhs[2 * n_dc, :, :GRID_W] = MASK_VALUE
    rhs[2 * n_dc + 1, :, GRID_W:] = MASK_VALUE
    rhs[2 * n_dc + 2] = np.tile(np.where(col_in, 0.0, MASK_VALUE), (1, 2))

    n_pairs = 3 * R // 2
    dr_idx = np.full((3, R, n_pairs, 2), n_dr, np.int32)
    for v, b_abs in enumerate((0, 1, n_blocks - 1)):
        for rq in range(R):
            r = R * b_abs + rq
            rs = min(max(r - NA_KH // 2, 0), n_rows - NA_KH)
            for piece in range(3 * R):
                krow = R * (b_abs - 1) + piece
                if 0 <= krow < n_rows and rs <= krow < rs + NA_KH:
                    dr_idx[v, rq, piece // 2, piece % 2] = krow - r + NA_KH - 1
    rows = jnp.concatenate([rpb.astype(f32) * LOG2E, jnp.zeros((H, 1, n_dc), f32)], axis=1)
    picked = jnp.take(rows, jnp.asarray(dr_idx.reshape(-1)), axis=1)
    picked = picked.reshape(H, 3 * R * n_pairs, 2 * n_dc)
    outside = jnp.asarray((dr_idx == n_dr).astype(np.float32).reshape(3 * R * n_pairs, 2))
    feats = jnp.concatenate([picked, jnp.broadcast_to(outside, (H,) + outside.shape),
                             jnp.ones((H, 3 * R * n_pairs, 1), f32)], axis=-1)
    feats = jnp.swapaxes(feats.reshape(H, 3, R * n_pairs, n_feat), 0, 1)
    tiles = jnp.dot(feats.reshape(-1, n_feat), jnp.asarray(rhs.reshape(n_feat, -1)),
                    precision=lax.Precision.HIGHEST)
    return tiles.reshape(3, H, R, n_pairs, GRID_W, 2 * GRID_W)


def _na_kernel(q_ref, kp_ref, kc_ref, kn_ref, vp_ref, vc_ref, vn_ref, b_ref, o_ref, *, heads):
    for h in range(heads):
        hs = slice(h * HEAD_DIM, (h + 1) * HEAD_DIM)
        q = q_ref[:, hs]
        k = jnp.concatenate([kp_ref[:, hs], kc_ref[:, hs], kn_ref[:, hs]], axis=0)
        v = jnp.concatenate([vp_ref[:, hs], vc_ref[:, hs], vn_ref[:, hs]], axis=0)
        v_ext = jnp.concatenate([v, jnp.ones_like(v)], axis=1)
        s = lax.dot_general(q, k, (((1,), (1,)), ((), ())), preferred_element_type=f32)
        bias = jnp.concatenate(
            [jnp.concatenate([b_ref[0, h, rq, pp] for pp in range(b_ref.shape[3])], axis=1)
             for rq in range(b_ref.shape[2])], axis=0)
        s = s + bias
        m = jnp.max(s, axis=-1, keepdims=True)
        p = jnp.exp2(s - m)
        o = jnp.dot(p.astype(bf16), v_ext, preferred_element_type=f32)
        o_ref[:, hs] = (o[:, :HEAD_DIM] / o[:, HEAD_DIM:HEAD_DIM + 1]).astype(o_ref.dtype)


def _na_attention(proj, bias, *, heads, na_width):
    S = proj.shape[0]
    tq = NA_ROWS_PER_BLOCK * GRID_W
    nb = S // tq
    assert S % tq == 0 and nb >= 3 and na_width == heads * HEAD_DIM

    def variant(b):
        return jnp.where(b == 0, 0, jnp.where(b == nb - 1, 2, 1))

    def blk(col, shift):
        return pl.BlockSpec((tq, na_width), lambda b: (jnp.clip(b + shift, 0, nb - 1), col))

    return pl.pallas_call(
        functools.partial(_na_kernel, heads=heads),
        grid=(nb,),
        in_specs=[blk(0, 0), blk(1, -1), blk(1, 0), blk(1, 1), blk(2, -1), blk(2, 0), blk(2, 1),
                  pl.BlockSpec((1,) + bias.shape[1:], lambda b: (variant(b), 0, 0, 0, 0, 0))],
        out_specs=pl.BlockSpec((tq, na_width), lambda b: (b, 0)),
        out_shape=jax.ShapeDtypeStruct((S, na_width), bf16),
        compiler_params=_params("arbitrary"),
        name="na_attn",
    )(proj, proj, proj, proj, proj, proj, proj, bias)


SOFTMAX_CHUNK = 64


def _gqa_kernel(q_ref, k_ref, vt_ref, o_ref, m_ref, acc_ref, s0_ref, s1_ref, p0_ref, p1_ref, *, heads, group):
    j = pl.program_id(1)
    vrows = vt_ref.shape[0] // (heads // group)

    @pl.when(j == 0)
    def _():
        m_ref[...] = jnp.full(m_ref.shape, MASK_VALUE, f32)
        acc_ref[...] = jnp.zeros(acc_ref.shape, f32)

    tk, tq = s0_ref.shape
    s_bufs = (s0_ref, s1_ref)
    p_bufs = (p0_ref, p1_ref)
    n_chunks = tk // SOFTMAX_CHUNK

    def scores(g):
        hs = slice(g * HEAD_DIM, (g + 1) * HEAD_DIM)
        ks = slice((g // group) * HEAD_DIM, (g // group + 1) * HEAD_DIM)
        s_ref = s_bufs[g % 2]
        s_ref[...] = lax.dot_general(k_ref[:, ks], q_ref[:, hs], (((1,), (1,)), ((), ())),
                                     preferred_element_type=f32)
        pmax = None
        for c in range(n_chunks):
            chunk = s_ref[c * SOFTMAX_CHUNK:(c + 1) * SOFTMAX_CHUNK, :]
            cm = jnp.max(chunk.reshape(SOFTMAX_CHUNK // F32_ROWS, F32_ROWS, tq), axis=0)
            pmax = cm if pmax is None else jnp.maximum(pmax, cm)
        return jnp.max(pmax, axis=0, keepdims=True)

    def accumulate(g, m_cur):
        s_ref, p_ref = s_bufs[g % 2], p_bufs[g % 2]
        m_prev = m_ref[g]
        m_new = jnp.maximum(m_prev, m_cur)
        alpha = jnp.exp2(m_prev - m_new)
        for c in range(n_chunks):
            rows = slice(c * SOFTMAX_CHUNK, (c + 1) * SOFTMAX_CHUNK)
            p_ref[rows, :] = jnp.exp2(s_ref[rows, :] - m_new).astype(bf16)
        vt = vt_ref[(g // group) * vrows:(g // group + 1) * vrows, :]
        acc_ref[g] = alpha * acc_ref[g] + jnp.dot(vt, p_ref[...], preferred_element_type=f32)
        m_ref[g] = m_new

    m_cur = scores(0)
    for g in range(heads):
        m_next = scores(g + 1) if g + 1 < heads else None
        accumulate(g, m_cur)
        m_cur = m_next

    @pl.when(j == pl.num_programs(1) - 1)
    def _():
        for g in range(heads):
            hs = slice(g * HEAD_DIM, (g + 1) * HEAD_DIM)
            acc = acc_ref[g]
            o_ref[:, hs] = (acc[:HEAD_DIM] / acc[HEAD_DIM:HEAD_DIM + 1]).T.astype(o_ref.dtype)


def _gqa_attention(proj, vt_ext, *, q_col, k_col, q_heads, kv_heads):
    S = proj.shape[0]
    group = q_heads // kv_heads
    qw, kw = q_heads * HEAD_DIM, kv_heads * HEAD_DIM
    vrows = vt_ext.shape[0] // kv_heads
    tq, tk = 512, 4096
    assert S % tq == 0 and S % tk == 0 and q_col % qw == 0 and k_col % kw == 0
    return pl.pallas_call(
        functools.partial(_gqa_kernel, heads=q_heads, group=group),
        grid=(S // tq, S // tk),
        in_specs=[
            pl.BlockSpec((tq, qw), lambda i, j: (i, q_col // qw)),
            pl.BlockSpec((tk, kw), lambda i, j: (j, k_col // kw)),
            pl.BlockSpec((kv_heads * vrows, tk), lambda i, j: (0, j)),
        ],
        out_specs=pl.BlockSpec((tq, qw), lambda i, j: (i, 0)),
        out_shape=jax.ShapeDtypeStruct((S, qw), bf16),
        scratch_shapes=[pltpu.VMEM((q_heads, 1, tq), f32), pltpu.VMEM((q_heads, vrows, tq), f32),
                        pltpu.VMEM((tk, tq), f32), pltpu.VMEM((tk, tq), f32),
                        pltpu.VMEM((tk, tq), bf16), pltpu.VMEM((tk, tq), bf16)],
        compiler_params=_params("parallel", "arbitrary"),
        name="gqa_attn",
    )(proj, proj, vt_ext)


def _out_proj_kernel(na_ref, gq_ref, nag_ref, gqg_ref, w_ref, x_ref, o_ref, wb_ref, *, na_width):
    @pl.when(pl.program_id(0) == 0)
    def _():
        wb_ref[...] = w_ref[...].astype(bf16)

    a = _rms(na_ref[...].astype(f32), nag_ref[...]).astype(bf16)
    b = _rms(gq_ref[...].astype(f32), gqg_ref[...]).astype(bf16)
    o = jnp.dot(a, wb_ref[:na_width, :], preferred_element_type=f32)
    o = o + jnp.dot(b, wb_ref[na_width:, :], preferred_element_type=f32)
    o_ref[...] = x_ref[...] + o


def _out_proj(na_o, gqa_o, na_g, gqa_g, w_out, x2):
    S, D = x2.shape
    na_width, gqa_width = na_o.shape[1], gqa_o.shape[1]
    tm = 512
    assert S % tm == 0
    return pl.pallas_call(
        functools.partial(_out_proj_kernel, na_width=na_width),
        grid=(S // tm,),
        in_specs=[
            pl.BlockSpec((tm, na_width), lambda i: (i, 0)),
            pl.BlockSpec((tm, gqa_width), lambda i: (i, 0)),
            pl.BlockSpec((1, na_width), lambda i: (0, 0)),
            pl.BlockSpec((1, gqa_width), lambda i: (0, 0)),
            pl.BlockSpec((na_width + gqa_width, D), lambda i: (0, 0), pipeline_mode=pl.Buffered(1)),
            pl.BlockSpec((tm, D), lambda i: (i, 0)),
        ],
        out_specs=pl.BlockSpec((tm, D), lambda i: (i, 0)),
        out_shape=jax.ShapeDtypeStruct((S, D), f32),
        scratch_shapes=[pltpu.VMEM((na_width + gqa_width, D), bf16)],
        compiler_params=_params("arbitrary"),
        name="out_proj",
    )(na_o, gqa_o, na_g, gqa_g, w_out, x2)


def _ffn_up_kernel(x_ref, xp_ref, xn_ref, g_ref, wg_ref, wv_ref, cwg_ref, cwv_ref, cbg_ref, cbv_ref,
                   o_ref, h_ref, pg_ref, pv_ref, *, tm):
    i = pl.program_id(0)
    j = pl.program_id(1)
    halo = BF16_ROWS

    @pl.when(j == 0)
    def _():
        g = g_ref[...]
        h_ref[halo:halo + tm, :] = _rms(x_ref[...], g).astype(bf16)
        zeros = jnp.zeros((F32_ROWS, x_ref.shape[1]), f32)
        prev = jnp.where(i > 0, _rms(xp_ref[...], g), 0.0)
        nxt = jnp.where(i < pl.num_programs(0) - 1, _rms(xn_ref[...], g), 0.0)
        h_ref[0:halo, :] = jnp.concatenate([zeros, prev], axis=0).astype(bf16)
        h_ref[halo + tm:, :] = jnp.concatenate([nxt, zeros], axis=0).astype(bf16)

    h = h_ref[...]
    pg_ref[...] = jnp.dot(h, wg_ref[...].astype(bf16), preferred_element_type=f32)
    pv_ref[...] = jnp.dot(h, wv_ref[...].astype(bf16), preferred_element_type=f32)

    def conv(p_ref, cw_ref, cb_ref):
        cw = cw_ref[...]
        p = p_ref[...]
        rows = p.shape[0]
        below = pltpu.roll(p * cw[0:1], 1, 0)
        above = pltpu.roll(p * cw[2:3], rows - 1, 0)
        return (below + p * cw[1:2] + above)[halo:halo + tm] + cb_ref[...]

    gate = conv(pg_ref, cwg_ref, cbg_ref)
    val = conv(pv_ref, cwv_ref, cbv_ref)
    o_ref[...] = (gate / (1.0 + jnp.exp(-gate)) * val).astype(o_ref.dtype)


def _ffn_up(x1, ln_g, w_up, conv_w, conv_b, d_ff):
    S, D = x1.shape
    tm, tn = 1024, 512
    assert S % tm == 0 and d_ff % tn == 0 and tm % F32_ROWS == 0
    nj = d_ff // tn
    rows8 = tm // F32_ROWS
    last8 = S // F32_ROWS - 1
    p_scratch = pltpu.VMEM((tm + 2 * BF16_ROWS, tn), f32)
    return pl.pallas_call(
        functools.partial(_ffn_up_kernel, tm=tm),
        grid=(S // tm, nj),
        in_specs=[
            pl.BlockSpec((tm, D), lambda i, j: (i, 0)),
            pl.BlockSpec((F32_ROWS, D), lambda i, j: (jnp.maximum(i * rows8 - 1, 0), 0)),
            pl.BlockSpec((F32_ROWS, D), lambda i, j: (jnp.minimum((i + 1) * rows8, last8), 0)),
            pl.BlockSpec((1, D), lambda i, j: (0, 0)),
            pl.BlockSpec((D, tn), lambda i, j: (0, j)),
            pl.BlockSpec((D, tn), lambda i, j: (0, j + nj)),
            pl.BlockSpec((CONV_W, tn), lambda i, j: (0, j)),
            pl.BlockSpec((CONV_W, tn), lambda i, j: (0, j + nj)),
            pl.BlockSpec((1, tn), lambda i, j: (0, j)),
            pl.BlockSpec((1, tn), lambda i, j: (0, j + nj)),
        ],
        out_specs=pl.BlockSpec((tm, tn), lambda i, j: (i, j)),
        out_shape=jax.ShapeDtypeStruct((S, d_ff), bf16),
        scratch_shapes=[pltpu.VMEM((tm + 2 * BF16_ROWS, D), bf16), p_scratch, p_scratch],
        compiler_params=_params("parallel", "arbitrary"),
        name="ffn_up",
    )(x1, x1, x1, ln_g, w_up, w_up, conv_w, conv_w, conv_b, conv_b)


def _ffn_down_kernel(a_ref, w_ref, x_ref, g_ref, o_ref, *, final_norm):
    y = x_ref[...] + jnp.dot(a_ref[...], w_ref[...], preferred_element_type=f32)
    o_ref[...] = _rms(y, g_ref[...]) if final_norm else y


def _ffn_down(act, w_down, x1, final_g, *, final_norm):
    S, D = x1.shape
    d_ff = act.shape[1]
    tm = 512
    assert S % tm == 0
    return pl.pallas_call(
        functools.partial(_ffn_down_kernel, final_norm=final_norm),
        grid=(S // tm,),
        in_specs=[
            pl.BlockSpec((tm, d_ff), lambda i: (i, 0)),
            pl.BlockSpec((d_ff, D), lambda i: (0, 0), pipeline_mode=pl.Buffered(1)),
            pl.BlockSpec((tm, D), lambda i: (i, 0)),
            pl.BlockSpec((1, D), lambda i: (0, 0)),
        ],
        out_specs=pl.BlockSpec((tm, D), lambda i: (i, 0)),
        out_shape=jax.ShapeDtypeStruct((S, D), f32),
        compiler_params=_params("arbitrary"),
        name="ffn_down",
    )(act, w_down, x1, final_g)


def _rope_tables(S):
    t = np.arange(S)
    row = (t // GRID_W).astype(np.float32)
    col = (t % GRID_W).astype(np.float32)
    axis_dim = HEAD_DIM // 2
    expo = np.arange(0, axis_dim, 2, dtype=np.float32) / np.float32(axis_dim)
    inv_freq = (np.float32(1.0) / np.power(np.float32(ROPE_THETA), expo)).astype(np.float32)
    ang = np.concatenate([row[:, None] * inv_freq[None, :], col[:, None] * inv_freq[None, :]], axis=-1)
    cos, sin = np.cos(ang).astype(np.float32), np.sin(ang).astype(np.float32)
    cos_t = np.repeat(cos, 2, axis=-1)
    sin_t = np.stack([-sin, sin], axis=-1).reshape(S, HEAD_DIM)
    return jnp.asarray(cos_t), jnp.asarray(sin_t)


def _in_proj_columns(q_g, k_g, *, na_width, gqa_width, kv_width, scale):
    q_heads, k_heads = gqa_width // HEAD_DIM, kv_width // HEAD_DIM
    ones = lambda n: jnp.ones((n,), f32)
    zeros = lambda n: jnp.zeros((n,), f32)
    gain = jnp.concatenate([ones(3 * na_width), jnp.tile(q_g.astype(f32), q_heads),
                            jnp.tile(k_g.astype(f32), k_heads), ones(kv_width)])
    rope = jnp.concatenate([zeros(3 * na_width), ones(gqa_width + kv_width), zeros(kv_width)])
    col_scale = jnp.concatenate([jnp.full((na_width,), scale * LOG2E, f32), ones(2 * na_width),
                                 jnp.full((gqa_width,), scale * LOG2E, f32), ones(2 * kv_width)])
    return gain[None], rope[None], col_scale[None]


def kernel(x, ln1_g, w_in, na_rpb, q_norm_g, k_norm_g, na_out_g, gqa_out_g, w_out, ln2_g, w_up,
           conv_w, conv_b, w_down, final_g):
    B, S, D = x.shape
    depth = w_in.shape[0]
    assert B == 1 and S % GRID_W == 0
    na_heads = na_rpb.shape[1]
    na_width = na_heads * HEAD_DIM
    gqa_width = gqa_out_g.shape[1]
    q_heads = gqa_width // HEAD_DIM
    kv_width = (w_in.shape[2] - 3 * na_width - gqa_width) // 2
    kv_heads = kv_width // HEAD_DIM
    d_ff = w_down.shape[1]
    scale = HEAD_DIM ** -0.5
    n_rows = S // GRID_W

    cos_t, sin_t = _rope_tables(S)
    xs = x.reshape(S, D)
    for l in range(depth):
        col_gain, col_rope, col_scale = _in_proj_columns(
            q_norm_g[l], k_norm_g[l], na_width=na_width, gqa_width=gqa_width, kv_width=kv_width, scale=scale)
        proj = _in_proj(xs, ln1_g[l][None], w_in[l], col_gain, col_rope, col_scale, cos_t, sin_t)
        bias = _na_bias_tables(na_rpb[l], n_rows)
        na_o = _na_attention(proj, bias, heads=na_heads, na_width=na_width)
        vt = proj[:, 3 * na_width + gqa_width + kv_width:].T.reshape(kv_heads, HEAD_DIM, S)
        ones = jnp.ones((kv_heads, BF16_ROWS, S), bf16)
        vt_ext = jnp.concatenate([vt, ones], axis=1).reshape(kv_heads * (HEAD_DIM + BF16_ROWS), S)
        gqa_o = _gqa_attention(proj, vt_ext, q_col=3 * na_width, k_col=3 * na_width + gqa_width,
                               q_heads=q_heads, kv_heads=kv_heads)
        x1 = _out_proj(na_o, gqa_o, na_out_g[l][None], gqa_out_g[l][None], w_out[l], xs)
        act = _ffn_up(x1, ln2_g[l][None], w_up[l], conv_w[l], conv_b[l][None], d_ff)
        xs = _ffn_down(act, w_down[l].astype(bf16), x1, final_g[None], final_norm=(l == depth - 1))
    return xs.reshape(B, S, D)
```

```python
import functools

import numpy as np
import jax
import jax.numpy as jnp
from jax import lax
from jax.experimental import pallas as pl
from jax.experimental.pallas import tpu as pltpu

HEAD_DIM = 128
GRID_W = 64
NA_KH = 8
NA_KW = 16
ROPE_THETA = 10000.0
EPS = 1e-6
CONV_W = 3
MASK_VALUE = -1e30
LANES = 128
BF16_ROWS = 16
F32_ROWS = 8
VMEM_LIMIT = 56 * 1024 * 1024
LOG2E = 1.4426950408889634

bf16 = jnp.bfloat16
f32 = jnp.float32


def _rms(x, g):
    ms = jnp.mean(x * x, axis=-1, keepdims=True)
    return x * lax.rsqrt(ms + EPS) * g


def _params(*sem):
    return pltpu.CompilerParams(dimension_semantics=sem, vmem_limit_bytes=VMEM_LIMIT)


def _norm_rope(a, gain, cos, sin_signed):
    an = _rms(a, gain)
    lane = lax.broadcasted_iota(jnp.int32, an.shape, 1)
    partner = jnp.where(lane % 2 == 0, pltpu.roll(an, LANES - 1, 1), pltpu.roll(an, 1, 1))
    return an * cos + partner * sin_signed


def _pipelined_steps(j, nj, produce, consume, consume_last, bufs):
    @pl.when(j == 0)
    def _():
        produce(bufs[0])

    for par in (0, 1):
        @pl.when((j > 0) & (j < nj) & (j % 2 == par))
        def _():
            produce(bufs[par])
            consume(bufs[1 - par])

    @pl.when(j == nj)
    def _():
        consume_last(bufs[(nj - 1) % 2])


def _in_proj_kernel(x_ref, g_ref, w_ref, gain_ref, rope_ref, scale_ref, cos_ref, sin_ref, o_ref,
                    h_ref, acc0_ref, acc1_ref, *, nj):
    j = pl.program_id(1)

    @pl.when(j == 0)
    def _():
        h_ref[...] = _rms(x_ref[...], g_ref[...]).astype(bf16)

    def produce(acc_ref):
        acc_ref[...] = jnp.dot(h_ref[...], w_ref[...].astype(bf16), preferred_element_type=f32)

    def consume(acc_ref):
        for a in range(o_ref.shape[1] // HEAD_DIM):
            sl = slice(a * HEAD_DIM, (a + 1) * HEAD_DIM)
            acc = acc_ref[:, sl]
            roped = _norm_rope(acc, gain_ref[:, sl], cos_ref[...], sin_ref[...])
            o_ref[:, sl] = (jnp.where(rope_ref[:, sl] > 0.0, roped, acc) * scale_ref[:, sl]).astype(o_ref.dtype)

    def consume_plain(acc_ref):
        o_ref[...] = (acc_ref[...] * scale_ref[...]).astype(o_ref.dtype)

    _pipelined_steps(j, nj, produce, consume, consume_plain, (acc0_ref, acc1_ref))


def _in_proj(x2, ln_g, w_in, col_gain, col_rope, col_scale, cos_t, sin_t, *, rope_start, rope_end):
    S, D = x2.shape
    N = w_in.shape[1]
    tm, tn = 1024, 512
    assert S % tm == 0 and N % tn == 0 and rope_start % tn == 0
    nj = N // tn
    first = rope_start // tn
    last_tile = (first + nj - 1) % nj
    assert (last_tile + 1) * tn <= rope_start or last_tile * tn >= rope_end

    def cur(j):
        return (jnp.minimum(j, nj - 1) + first) % nj

    def prev(j):
        return (jnp.maximum(j - 1, 0) + first) % nj

    return pl.pallas_call(
        functools.partial(_in_proj_kernel, nj=nj),
        grid=(S // tm, nj + 1),
        in_specs=[
            pl.BlockSpec((tm, D), lambda i, j: (i, 0)),
            pl.BlockSpec((1, D), lambda i, j: (0, 0)),
            pl.BlockSpec((D, tn), lambda i, j: (0, cur(j))),
            pl.BlockSpec((1, tn), lambda i, j: (0, prev(j))),
            pl.BlockSpec((1, tn), lambda i, j: (0, prev(j))),
            pl.BlockSpec((1, tn), lambda i, j: (0, prev(j))),
            pl.BlockSpec((tm, HEAD_DIM), lambda i, j: (i, 0)),
            pl.BlockSpec((tm, HEAD_DIM), lambda i, j: (i, 0)),
        ],
        out_specs=pl.BlockSpec((tm, tn), lambda i, j: (i, prev(j))),
        out_shape=jax.ShapeDtypeStruct((S, N), bf16),
        scratch_shapes=[pltpu.VMEM((tm, D), bf16), pltpu.VMEM((tm, tn), f32), pltpu.VMEM((tm, tn), f32)],
        compiler_params=_params("parallel", "arbitrary"),
        name="in_proj",
    )(x2, ln_g, w_in, col_gain, col_rope, col_scale, cos_t, sin_t)


NA_ROWS_PER_BLOCK = 4


def _na_bias_tables(rpb, n_rows):
    H, n_dr, n_dc = rpb.shape
    R = NA_ROWS_PER_BLOCK
    n_blocks = n_rows // R
    c = np.arange(GRID_W)
    col_start = np.clip(c - NA_KW // 2, 0, GRID_W - NA_KW)
    col_in = (c[None, :] >= col_start[:, None]) & (c[None, :] < col_start[:, None] + NA_KW)
    dc = np.clip(c[None, :] - c[:, None], -(NA_KW - 1), NA_KW - 1) + (NA_KW - 1)
    onehot = ((dc[None] == np.arange(n_dc)[:, None, None]) & col_in[None]).astype(np.float32)

    n_feat = 2 * n_dc + 3
    rhs = np.zeros((n_feat, GRID_W, 2 * GRID_W), np.float32)
    rhs[0:n_dc, :, :GRID_W] = onehot
    rhs[n_dc:2 * n_dc, :, GRID_W:] = onehot
    rhs[2 * n_dc, :, :GRID_W] = MASK_VALUE
    rhs[2 * n_dc + 1, :, GRID_W:] = MASK_VALUE
    rhs[2 * n_dc + 2] = np.tile(np.where(col_in, 0.0, MASK_VALUE), (1, 2))

    n_pairs = 3 * R // 2
    dr_idx = np.full((3, R, n_pairs, 2), n_dr, np.int32)
    for v, b_abs in enumerate((0, 1, n_blocks - 1)):
        for rq in range(R):
            r = R * b_abs + rq
            rs = min(max(r - NA_KH // 2, 0), n_rows - NA_KH)
            for piece in range(3 * R):
                krow = R * (b_abs - 1) + piece
                if 0 <= krow < n_rows and rs <= krow < rs + NA_KH:
                    dr_idx[v, rq, piece // 2, piece % 2] = krow - r + NA_KH - 1
    rows = jnp.concatenate([rpb.astype(f32) * LOG2E, jnp.zeros((H, 1, n_dc), f32)], axis=1)
    picked = jnp.take(rows, jnp.asarray(dr_idx.reshape(-1)), axis=1)
    picked = picked.reshape(H, 3 * R * n_pairs, 2 * n_dc)
    outside = jnp.asarray((dr_idx == n_dr).astype(np.float32).reshape(3 * R * n_pairs, 2))
    feats = jnp.concatenate([picked, jnp.broadcast_to(outside, (H,) + outside.shape),
                             jnp.ones((H, 3 * R * n_pairs, 1), f32)], axis=-1)
    feats = jnp.swapaxes(feats.reshape(H, 3, R * n_pairs, n_feat), 0, 1)
    tiles = jnp.dot(feats.reshape(-1, n_feat), jnp.asarray(rhs.reshape(n_feat, -1)),
                    precision=lax.Precision.HIGHEST)
    return tiles.reshape(3, H, R, n_pairs, GRID_W, 2 * GRID_W)


def _na_kernel(q_ref, kp_ref, kc_ref, kn_ref, vp_ref, vc_ref, vn_ref, b_ref, o_ref, *, heads):
    for h in range(heads):
        hs = slice(h * HEAD_DIM, (h + 1) * HEAD_DIM)
        q = q_ref[:, hs]
        k = jnp.concatenate([kp_ref[:, hs], kc_ref[:, hs], kn_ref[:, hs]], axis=0)
        v = jnp.concatenate([vp_ref[:, hs], vc_ref[:, hs], vn_ref[:, hs]], axis=0)
        v_ext = jnp.concatenate([v, jnp.ones_like(v)], axis=1)
        s = lax.dot_general(q, k, (((1,), (1,)), ((), ())), preferred_element_type=f32)
        bias = jnp.concatenate(
            [jnp.concatenate([b_ref[0, h, rq, pp] for pp in range(b_ref.shape[3])], axis=1)
             for rq in range(b_ref.shape[2])], axis=0)
        s = s + bias
        m = jnp.max(s, axis=-1, keepdims=True)
        p = jnp.exp2(s - m)
        o = jnp.dot(p.astype(bf16), v_ext, preferred_element_type=f32)
        o_ref[:, hs] = (o[:, :HEAD_DIM] / o[:, HEAD_DIM:HEAD_DIM + 1]).astype(o_ref.dtype)


def _na_attention(proj, bias, *, heads, na_width):
    S = proj.shape[0]
    tq = NA_ROWS_PER_BLOCK * GRID_W
    nb = S // tq
    assert S % tq == 0 and nb >= 3 and na_width == heads * HEAD_DIM

    def variant(b):
        return jnp.where(b == 0, 0, jnp.where(b == nb - 1, 2, 1))

    def blk(col, shift):
        return pl.BlockSpec((tq, na_width), lambda b: (jnp.clip(b + shift, 0, nb - 1), col))

    return pl.pallas_call(
        functools.partial(_na_kernel, heads=heads),
        grid=(nb,),
        in_specs=[blk(0, 0), blk(1, -1), blk(1, 0), blk(1, 1), blk(2, -1), blk(2, 0), blk(2, 1),
                  pl.BlockSpec((1,) + bias.shape[1:], lambda b: (variant(b), 0, 0, 0, 0, 0))],
        out_specs=pl.BlockSpec((tq, na_width), lambda b: (b, 0)),
        out_shape=jax.ShapeDtypeStruct((S, na_width), bf16),
        compiler_params=_params("arbitrary"),
        name="na_attn",
    )(proj, proj, proj, proj, proj, proj, proj, bias)


SOFTMAX_CHUNK = 64
GQA_SUB_BLOCKS = 1


def _gqa_kernel(q_ref, k_ref, vt_ref, o_ref, m_ref, acc_ref, s0_ref, s1_ref, p0_ref, p1_ref, *, heads, group):
    j = pl.program_id(1)

    @pl.when(j == 0)
    def _():
        m_ref[...] = jnp.full(m_ref.shape, MASK_VALUE, f32)
        acc_ref[...] = jnp.zeros(acc_ref.shape, f32)

    ts, tq = s0_ref.shape
    s_bufs = (s0_ref, s1_ref)
    p_bufs = (p0_ref, p1_ref)
    n_chunks = ts // SOFTMAX_CHUNK
    units = [(b, g) for b in range(k_ref.shape[0] // ts) for g in range(heads)]

    def scores(u):
        b, g = units[u]
        hs = slice(g * HEAD_DIM, (g + 1) * HEAD_DIM)
        ks = slice((g // group) * HEAD_DIM, (g // group + 1) * HEAD_DIM)
        s_ref = s_bufs[u % 2]
        s_ref[...] = lax.dot_general(k_ref[b * ts:(b + 1) * ts, ks], q_ref[:, hs], (((1,), (1,)), ((), ())),
                                     preferred_element_type=f32)
        pmax = None
        for c in range(n_chunks):
            chunk = s_ref[c * SOFTMAX_CHUNK:(c + 1) * SOFTMAX_CHUNK, :]
            cm = jnp.max(chunk.reshape(SOFTMAX_CHUNK // F32_ROWS, F32_ROWS, tq), axis=0)
            pmax = cm if pmax is None else jnp.maximum(pmax, cm)
        return jnp.max(pmax, axis=0, keepdims=True)

    def accumulate(u, m_cur):
        b, g = units[u]
        s_ref, p_ref = s_bufs[u % 2], p_bufs[u % 2]
        m_prev = m_ref[g]
        m_new = jnp.maximum(m_prev, m_cur)
        alpha = jnp.exp2(m_prev - m_new)
        for c in range(n_chunks):
            rows = slice(c * SOFTMAX_CHUNK, (c + 1) * SOFTMAX_CHUNK)
            p_ref[rows, :] = jnp.exp2(s_ref[rows, :] - m_new).astype(bf16)
        vt = vt_ref[(g // group) * HEAD_DIM:(g // group + 1) * HEAD_DIM, b * ts:(b + 1) * ts]
        vt_ext = jnp.concatenate([vt, jnp.ones((BF16_ROWS, ts), bf16)], axis=0)
        acc_ref[g] = alpha * acc_ref[g] + jnp.dot(vt_ext, p_ref[...], preferred_element_type=f32)
        m_ref[g] = m_new

    m_cur = scores(0)
    for u in range(len(units)):
        m_next = scores(u + 1) if u + 1 < len(units) else None
        accumulate(u, m_cur)
        m_cur = m_next

    @pl.when(j == pl.num_programs(1) - 1)
    def _():
        for g in range(heads):
            hs = slice(g * HEAD_DIM, (g + 1) * HEAD_DIM)
            acc = acc_ref[g]
            o_ref[:, hs] = (acc[:HEAD_DIM] / acc[HEAD_DIM:HEAD_DIM + 1]).T.astype(o_ref.dtype)


def _gqa_attention(proj, vt, *, q_col, k_col, q_heads, kv_heads):
    S = proj.shape[0]
    group = q_heads // kv_heads
    qw, kw = q_heads * HEAD_DIM, kv_heads * HEAD_DIM
    vrows = HEAD_DIM + BF16_ROWS
    tq, tk = 512, 4096
    ts = tk // GQA_SUB_BLOCKS
    assert S % tq == 0 and S % tk == 0 and q_col % qw == 0 and k_col % kw == 0 and ts % SOFTMAX_CHUNK == 0
    return pl.pallas_call(
        functools.partial(_gqa_kernel, heads=q_heads, group=group),
        grid=(S // tq, S // tk),
        in_specs=[
            pl.BlockSpec((tq, qw), lambda i, j: (i, q_col // qw)),
            pl.BlockSpec((tk, kw), lambda i, j: (j, k_col // kw)),
            pl.BlockSpec((kw, tk), lambda i, j: (0, j)),
        ],
        out_specs=pl.BlockSpec((tq, qw), lambda i, j: (i, 0)),
        out_shape=jax.ShapeDtypeStruct((S, qw), bf16),
        scratch_shapes=[pltpu.VMEM((q_heads, 1, tq), f32), pltpu.VMEM((q_heads, vrows, tq), f32),
                        pltpu.VMEM((ts, tq), f32), pltpu.VMEM((ts, tq), f32),
                        pltpu.VMEM((ts, tq), bf16), pltpu.VMEM((ts, tq), bf16)],
        compiler_params=_params("parallel", "arbitrary"),
        name="gqa_attn",
    )(proj, proj, vt)


def _out_proj_kernel(na_ref, gq_ref, nag_ref, gqg_ref, w_ref, x_ref, o_ref, wb_ref, *, na_width):
    @pl.when(pl.program_id(0) == 0)
    def _():
        wb_ref[...] = w_ref[...].astype(bf16)

    a = _rms(na_ref[...].astype(f32), nag_ref[...]).astype(bf16)
    b = _rms(gq_ref[...].astype(f32), gqg_ref[...]).astype(bf16)
    o = jnp.dot(a, wb_ref[:na_width, :], preferred_element_type=f32)
    o = o + jnp.dot(b, wb_ref[na_width:, :], preferred_element_type=f32)
    o_ref[...] = x_ref[...] + o


def _out_proj(na_o, gqa_o, na_g, gqa_g, w_out, x2):
    S, D = x2.shape
    na_width, gqa_width = na_o.shape[1], gqa_o.shape[1]
    tm = 512
    assert S % tm == 0
    return pl.pallas_call(
        functools.partial(_out_proj_kernel, na_width=na_width),
        grid=(S // tm,),
        in_specs=[
            pl.BlockSpec((tm, na_width), lambda i: (i, 0)),
            pl.BlockSpec((tm, gqa_width), lambda i: (i, 0)),
            pl.BlockSpec((1, na_width), lambda i: (0, 0)),
            pl.BlockSpec((1, gqa_width), lambda i: (0, 0)),
            pl.BlockSpec((na_width + gqa_width, D), lambda i: (0, 0), pipeline_mode=pl.Buffered(1)),
            pl.BlockSpec((tm, D), lambda i: (i, 0)),
        ],
        out_specs=pl.BlockSpec((tm, D), lambda i: (i, 0)),
        out_shape=jax.ShapeDtypeStruct((S, D), f32),
        scratch_shapes=[pltpu.VMEM((na_width + gqa_width, D), bf16)],
        compiler_params=_params("arbitrary"),
        name="out_proj",
    )(na_o, gqa_o, na_g, gqa_g, w_out, x2)


def _ffn_up_kernel(x_ref, xp_ref, xn_ref, g_ref, wg_ref, wv_ref, cwg_ref, cwv_ref, cbg_ref, cbv_ref,
                   o_ref, h_ref, pg_ref, pv_ref, *, tm):
    i = pl.program_id(0)
    j = pl.program_id(1)
    halo = BF16_ROWS

    @pl.when(j == 0)
    def _():
        g = g_ref[...]
        h_ref[halo:halo + tm, :] = _rms(x_ref[...], g).astype(bf16)
        zeros = jnp.zeros((F32_ROWS, x_ref.shape[1]), f32)
        prev = jnp.where(i > 0, _rms(xp_ref[...], g), 0.0)
        nxt = jnp.where(i < pl.num_programs(0) - 1, _rms(xn_ref[...], g), 0.0)
        h_ref[0:halo, :] = jnp.concatenate([zeros, prev], axis=0).astype(bf16)
        h_ref[halo + tm:, :] = jnp.concatenate([nxt, zeros], axis=0).astype(bf16)

    h = h_ref[...]
    pg_ref[...] = jnp.dot(h, wg_ref[...].astype(bf16), preferred_element_type=f32)
    pv_ref[...] = jnp.dot(h, wv_ref[...].astype(bf16), preferred_element_type=f32)

    def conv(p_ref, cw_ref, cb_ref):
        cw = cw_ref[...]
        p = p_ref[...]
        rows = p.shape[0]
        below = pltpu.roll(p * cw[0:1], 1, 0)
        above = pltpu.roll(p * cw[2:3], rows - 1, 0)
        return (below + p * cw[1:2] + above)[halo:halo + tm] + cb_ref[...]

    gate = conv(pg_ref, cwg_ref, cbg_ref)
    val = conv(pv_ref, cwv_ref, cbv_ref)
    o_ref[...] = (gate / (1.0 + jnp.exp(-gate)) * val).astype(o_ref.dtype)


def _ffn_up(x1, ln_g, w_up, conv_w, conv_b, d_ff):
    S, D = x1.shape
    tm, tn = 1024, 512
    assert S % tm == 0 and d_ff % tn == 0 and tm % F32_ROWS == 0
    nj = d_ff // tn
    rows8 = tm // F32_ROWS
    last8 = S // F32_ROWS - 1
    p_scratch = pltpu.VMEM((tm + 2 * BF16_ROWS, tn), f32)
    return pl.pallas_call(
        functools.partial(_ffn_up_kernel, tm=tm),
        grid=(S // tm, nj),
        in_specs=[
            pl.BlockSpec((tm, D), lambda i, j: (i, 0)),
            pl.BlockSpec((F32_ROWS, D), lambda i, j: (jnp.maximum(i * rows8 - 1, 0), 0)),
            pl.BlockSpec((F32_ROWS, D), lambda i, j: (jnp.minimum((i + 1) * rows8, last8), 0)),
            pl.BlockSpec((1, D), lambda i, j: (0, 0)),
            pl.BlockSpec((D, tn), lambda i, j: (0, j)),
            pl.BlockSpec((D, tn), lambda i, j: (0, j + nj)),
            pl.BlockSpec((CONV_W, tn), lambda i, j: (0, j)),
            pl.BlockSpec((CONV_W, tn), lambda i, j: (0, j + nj)),
            pl.BlockSpec((1, tn), lambda i, j: (0, j)),
            pl.BlockSpec((1, tn), lambda i, j: (0, j + nj)),
        ],
        out_specs=pl.BlockSpec((tm, tn), lambda i, j: (i, j)),
        out_shape=jax.ShapeDtypeStruct((S, d_ff), bf16),
        scratch_shapes=[pltpu.VMEM((tm + 2 * BF16_ROWS, D), bf16), p_scratch, p_scratch],
        compiler_params=_params("parallel", "arbitrary"),
        name="ffn_up",
    )(x1, x1, x1, ln_g, w_up, w_up, conv_w, conv_w, conv_b, conv_b)


def _ffn_down_kernel(a_ref, w_ref, x_ref, g_ref, o_ref, *, final_norm):
    y = x_ref[...] + jnp.dot(a_ref[...], w_ref[...], preferred_element_type=f32)
    o_ref[...] = _rms(y, g_ref[...]) if final_norm else y


def _ffn_down(act, w_down, x1, final_g, *, final_norm):
    S, D = x1.shape
    d_ff = act.shape[1]
    tm = 512
    assert S % tm == 0
    return pl.pallas_call(
        functools.partial(_ffn_down_kernel, final_norm=final_norm),
        grid=(S // tm,),
        in_specs=[
            pl.BlockSpec((tm, d_ff), lambda i: (i, 0)),
            pl.BlockSpec((d_ff, D), lambda i: (0, 0), pipeline_mode=pl.Buffered(1)),
            pl.BlockSpec((tm, D), lambda i: (i, 0)),
            pl.BlockSpec((1, D), lambda i: (0, 0)),
        ],
        out_specs=pl.BlockSpec((tm, D), lambda i: (i, 0)),
        out_shape=jax.ShapeDtypeStruct((S, D), f32),
        compiler_params=_params("arbitrary"),
        name="ffn_down",
    )(act, w_down, x1, final_g)


def _rope_tables(S):
    t = np.arange(S)
    row = (t // GRID_W).astype(np.float32)
    col = (t % GRID_W).astype(np.float32)
    axis_dim = HEAD_DIM // 2
    expo = np.arange(0, axis_dim, 2, dtype=np.float32) / np.float32(axis_dim)
    inv_freq = (np.float32(1.0) / np.power(np.float32(ROPE_THETA), expo)).astype(np.float32)
    ang = np.concatenate([row[:, None] * inv_freq[None, :], col[:, None] * inv_freq[None, :]], axis=-1)
    cos, sin = np.cos(ang).astype(np.float32), np.sin(ang).astype(np.float32)
    cos_t = np.repeat(cos, 2, axis=-1)
    sin_t = np.stack([-sin, sin], axis=-1).reshape(S, HEAD_DIM)
    return jnp.asarray(cos_t), jnp.asarray(sin_t)


def _in_proj_columns(q_g, k_g, *, na_width, gqa_width, kv_width, scale):
    q_heads, k_heads = gqa_width // HEAD_DIM, kv_width // HEAD_DIM
    ones = lambda n: jnp.ones((n,), f32)
    zeros = lambda n: jnp.zeros((n,), f32)
    gain = jnp.concatenate([ones(3 * na_width), jnp.tile(q_g.astype(f32), q_heads),
                            jnp.tile(k_g.astype(f32), k_heads), ones(kv_width)])
    rope = jnp.concatenate([zeros(3 * na_width), ones(gqa_width + kv_width), zeros(kv_width)])
    col_scale = jnp.concatenate([jnp.full((na_width,), scale * LOG2E, f32), ones(2 * na_width),
                                 jnp.full((gqa_width,), scale * LOG2E, f32), ones(2 * kv_width)])
    return gain[None], rope[None], col_scale[None]


def kernel(x, ln1_g, w_in, na_rpb, q_norm_g, k_norm_g, na_out_g, gqa_out_g, w_out, ln2_g, w_up,
           conv_w, conv_b, w_down, final_g):
    B, S, D = x.shape
    depth = w_in.shape[0]
    assert B == 1 and S % GRID_W == 0
    na_heads = na_rpb.shape[1]
    na_width = na_heads * HEAD_DIM
    gqa_width = gqa_out_g.shape[1]
    q_heads = gqa_width // HEAD_DIM
    kv_width = (w_in.shape[2] - 3 * na_width - gqa_width) // 2
    kv_heads = kv_width // HEAD_DIM
    d_ff = w_down.shape[1]
    scale = HEAD_DIM ** -0.5
    n_rows = S // GRID_W

    cos_t, sin_t = _rope_tables(S)
    xs = x.reshape(S, D)
    for l in range(depth):
        col_gain, col_rope, col_scale = _in_proj_columns(
            q_norm_g[l], k_norm_g[l], na_width=na_width, gqa_width=gqa_width, kv_width=kv_width, scale=scale)
        proj = _in_proj(xs, ln1_g[l][None], w_in[l], col_gain, col_rope, col_scale, cos_t, sin_t,
                        rope_start=3 * na_width, rope_end=3 * na_width + gqa_width + kv_width)
        bias = _na_bias_tables(na_rpb[l], n_rows)
        na_o = _na_attention(proj, bias, heads=na_heads, na_width=na_width)
        vt = proj[:, 3 * na_width + gqa_width + kv_width:].T
        gqa_o = _gqa_attention(proj, vt, q_col=3 * na_width, k_col=3 * na_width + gqa_width,
                               q_heads=q_heads, kv_heads=kv_heads)
        x1 = _out_proj(na_o, gqa_o, na_out_g[l][None], gqa_out_g[l][None], w_out[l], xs)
        act = _ffn_up(x1, ln2_g[l][None], w_up[l], conv_w[l], conv_b[l][None], d_ff)
        xs = _ffn_down(act, w_down[l].astype(bf16), x1, final_g[None], final_norm=(l == depth - 1))
    return xs.reshape(B, S, D)
```

```python
import functools

import numpy as np
import jax
import jax.numpy as jnp
from jax import lax
from jax.experimental import pallas as pl
from jax.experimental.pallas import tpu as pltpu

HEAD_DIM = 128
GRID_W = 64
NA_KH = 8
NA_KW = 16
ROPE_THETA = 10000.0
EPS = 1e-6
CONV_W = 3
MASK_VALUE = -1e30
LANES = 128
BF16_ROWS = 16
F32_ROWS = 8
VMEM_LIMIT = 56 * 1024 * 1024
LOG2E = 1.4426950408889634

bf16 = jnp.bfloat16
f32 = jnp.float32


def _rms(x, g):
    ms = jnp.mean(x * x, axis=-1, keepdims=True)
    return x * lax.rsqrt(ms + EPS) * g


def _params(*sem):
    return pltpu.CompilerParams(dimension_semantics=sem, vmem_limit_bytes=VMEM_LIMIT)


def _norm_rope(a, gain, cos, sin_signed):
    an = _rms(a, gain)
    lane = lax.broadcasted_iota(jnp.int32, an.shape, 1)
    partner = jnp.where(lane % 2 == 0, pltpu.roll(an, LANES - 1, 1), pltpu.roll(an, 1, 1))
    return an * cos + partner * sin_signed


def _pipelined_steps(j, nj, produce, consume, consume_last, bufs):
    @pl.when(j == 0)
    def _():
        produce(bufs[0])

    for par in (0, 1):
        @pl.when((j > 0) & (j < nj) & (j % 2 == par))
        def _():
            produce(bufs[par])
            consume(bufs[1 - par])

    @pl.when(j == nj)
    def _():
        consume_last(bufs[(nj - 1) % 2])


def _in_proj_kernel(x_ref, g_ref, w_ref, gain_ref, rope_ref, scale_ref, cos_ref, sin_ref, o_ref,
                    h_ref, acc0_ref, acc1_ref, *, nj):
    j = pl.program_id(1)

    @pl.when(j == 0)
    def _():
        h_ref[...] = _rms(x_ref[...], g_ref[...]).astype(bf16)

    def produce(acc_ref):
        acc_ref[...] = jnp.dot(h_ref[...], w_ref[...].astype(bf16), preferred_element_type=f32)

    def consume(acc_ref):
        for a in range(o_ref.shape[1] // HEAD_DIM):
            sl = slice(a * HEAD_DIM, (a + 1) * HEAD_DIM)
            acc = acc_ref[:, sl]
            roped = _norm_rope(acc, gain_ref[:, sl], cos_ref[...], sin_ref[...])
            o_ref[:, sl] = (jnp.where(rope_ref[:, sl] > 0.0, roped, acc) * scale_ref[:, sl]).astype(o_ref.dtype)

    def consume_plain(acc_ref):
        o_ref[...] = (acc_ref[...] * scale_ref[...]).astype(o_ref.dtype)

    _pipelined_steps(j, nj, produce, consume, consume_plain, (acc0_ref, acc1_ref))


def _in_proj(x2, ln_g, w_in, col_gain, col_rope, col_scale, cos_t, sin_t, *, rope_start, rope_end):
    S, D = x2.shape
    N = w_in.shape[1]
    tm, tn = 1024, 768
    assert S % tm == 0 and N % tn == 0 and rope_start % tn == 0
    nj = N // tn
    first = rope_start // tn
    last_tile = (first + nj - 1) % nj
    assert (last_tile + 1) * tn <= rope_start or last_tile * tn >= rope_end

    n_i = S // tm

    def cur(j):
        return (j % nj + first) % nj

    def prev(j):
        return (jnp.maximum(j - 1, 0) + first) % nj

    def x_block(i, j):
        return jnp.minimum(i + (j > 0).astype(jnp.int32), n_i - 1)

    return pl.pallas_call(
        functools.partial(_in_proj_kernel, nj=nj),
        grid=(n_i, nj + 1),
        in_specs=[
            pl.BlockSpec((tm, D), lambda i, j: (x_block(i, j), 0)),
            pl.BlockSpec((1, D), lambda i, j: (0, 0)),
            pl.BlockSpec((D, tn), lambda i, j: (0, cur(j))),
            pl.BlockSpec((1, tn), lambda i, j: (0, prev(j))),
            pl.BlockSpec((1, tn), lambda i, j: (0, prev(j))),
            pl.BlockSpec((1, tn), lambda i, j: (0, prev(j))),
            pl.BlockSpec((tm, HEAD_DIM), lambda i, j: (i, 0)),
            pl.BlockSpec((tm, HEAD_DIM), lambda i, j: (i, 0)),
        ],
        out_specs=pl.BlockSpec((tm, tn), lambda i, j: (i, prev(j))),
        out_shape=jax.ShapeDtypeStruct((S, N), bf16),
        scratch_shapes=[pltpu.VMEM((tm, D), bf16), pltpu.VMEM((tm, tn), f32), pltpu.VMEM((tm, tn), f32)],
        compiler_params=_params("parallel", "arbitrary"),
        name="in_proj",
    )(x2, ln_g, w_in, col_gain, col_rope, col_scale, cos_t, sin_t)


NA_ROWS_PER_BLOCK = 4


def _na_bias_tables(rpb, n_rows):
    H, n_dr, n_dc = rpb.shape
    R = NA_ROWS_PER_BLOCK
    n_blocks = n_rows // R
    c = np.arange(GRID_W)
    col_start = np.clip(c - NA_KW // 2, 0, GRID_W - NA_KW)
    col_in = (c[None, :] >= col_start[:, None]) & (c[None, :] < col_start[:, None] + NA_KW)
    dc = np.clip(c[None, :] - c[:, None], -(NA_KW - 1), NA_KW - 1) + (NA_KW - 1)
    onehot = ((dc[None] == np.arange(n_dc)[:, None, None]) & col_in[None]).astype(np.float32)

    n_feat = 2 * n_dc + 3
    rhs = np.zeros((n_feat, GRID_W, 2 * GRID_W), np.float32)
    rhs[0:n_dc, :, :GRID_W] = onehot
    rhs[n_dc:2 * n_dc, :, GRID_W:] = onehot
    rhs[2 * n_dc, :, :GRID_W] = MASK_VALUE
    rhs[2 * n_dc + 1, :, GRID_W:] = MASK_VALUE
    rhs[2 * n_dc + 2] = np.tile(np.where(col_in, 0.0, MASK_VALUE), (1, 2))

    n_pairs = 3 * R // 2
    dr_idx = np.full((3, R, n_pairs, 2), n_dr, np.int32)
    for v, b_abs in enumerate((0, 1, n_blocks - 1)):
        for rq in range(R):
            r = R * b_abs + rq
            rs = min(max(r - NA_KH // 2, 0), n_rows - NA_KH)
            for piece in range(3 * R):
                krow = R * (b_abs - 1) + piece
                if 0 <= krow < n_rows and rs <= krow < rs + NA_KH:
                    dr_idx[v, rq, piece // 2, piece % 2] = krow - r + NA_KH - 1
    rows = jnp.concatenate([rpb.astype(f32) * LOG2E, jnp.zeros((H, 1, n_dc), f32)], axis=1)
    picked = jnp.take(rows, jnp.asarray(dr_idx.reshape(-1)), axis=1)
    picked = picked.reshape(H, 3 * R * n_pairs, 2 * n_dc)
    outside = jnp.asarray((dr_idx == n_dr).astype(np.float32).reshape(3 * R * n_pairs, 2))
    feats = jnp.concatenate([picked, jnp.broadcast_to(outside, (H,) + outside.shape),
                             jnp.ones((H, 3 * R * n_pairs, 1), f32)], axis=-1)
    feats = jnp.swapaxes(feats.reshape(H, 3, R * n_pairs, n_feat), 0, 1)
    tiles = jnp.dot(feats.reshape(-1, n_feat), jnp.asarray(rhs.reshape(n_feat, -1)),
                    precision=lax.Precision.HIGHEST)
    return tiles.reshape(3, H, R, n_pairs, GRID_W, 2 * GRID_W)


def _na_kernel(q_ref, kp_ref, kc_ref, kn_ref, vp_ref, vc_ref, vn_ref, b_ref, o_ref, *, heads):
    for h in range(heads):
        hs = slice(h * HEAD_DIM, (h + 1) * HEAD_DIM)
        q = q_ref[:, hs]
        k = jnp.concatenate([kp_ref[:, hs], kc_ref[:, hs], kn_ref[:, hs]], axis=0)
        v = jnp.concatenate([vp_ref[:, hs], vc_ref[:, hs], vn_ref[:, hs]], axis=0)
        v_ext = jnp.concatenate([v, jnp.ones_like(v)], axis=1)
        s = lax.dot_general(q, k, (((1,), (1,)), ((), ())), preferred_element_type=f32)
        bias = jnp.concatenate(
            [jnp.concatenate([b_ref[0, h, rq, pp] for pp in range(b_ref.shape[3])], axis=1)
             for rq in range(b_ref.shape[2])], axis=0)
        s = s + bias
        m = jnp.max(s, axis=-1, keepdims=True)
        p = jnp.exp2(s - m)
        o = jnp.dot(p.astype(bf16), v_ext, preferred_element_type=f32)
        o_ref[:, hs] = (o[:, :HEAD_DIM] / o[:, HEAD_DIM:HEAD_DIM + 1]).astype(o_ref.dtype)


def _na_attention(proj, bias, *, heads, na_width):
    S = proj.shape[0]
    tq = NA_ROWS_PER_BLOCK * GRID_W
    nb = S // tq
    assert S % tq == 0 and nb >= 3 and na_width == heads * HEAD_DIM

    def variant(b):
        return jnp.where(b == 0, 0, jnp.where(b == nb - 1, 2, 1))

    def blk(col, shift):
        return pl.BlockSpec((tq, na_width), lambda b: (jnp.clip(b + shift, 0, nb - 1), col))

    return pl.pallas_call(
        functools.partial(_na_kernel, heads=heads),
        grid=(nb,),
        in_specs=[blk(0, 0), blk(1, -1), blk(1, 0), blk(1, 1), blk(2, -1), blk(2, 0), blk(2, 1),
                  pl.BlockSpec((1,) + bias.shape[1:], lambda b: (variant(b), 0, 0, 0, 0, 0))],
        out_specs=pl.BlockSpec((tq, na_width), lambda b: (b, 0)),
        out_shape=jax.ShapeDtypeStruct((S, na_width), bf16),
        compiler_params=_params("arbitrary"),
        name="na_attn",
    )(proj, proj, proj, proj, proj, proj, proj, bias)


SOFTMAX_CHUNK = 64
GQA_SUB_BLOCKS = 1


def _gqa_kernel(q_ref, k_ref, vt_ref, o_ref, m_ref, acc_ref, s0_ref, s1_ref, p0_ref, p1_ref, *, heads, group):
    j = pl.program_id(1)

    @pl.when(j == 0)
    def _():
        m_ref[...] = jnp.full(m_ref.shape, MASK_VALUE, f32)
        acc_ref[...] = jnp.zeros(acc_ref.shape, f32)

    ts, tq = s0_ref.shape
    s_bufs = (s0_ref, s1_ref)
    p_bufs = (p0_ref, p1_ref)
    n_chunks = ts // SOFTMAX_CHUNK
    units = [(b, g) for b in range(k_ref.shape[0] // ts) for g in range(heads)]

    def scores(u):
        b, g = units[u]
        hs = slice(g * HEAD_DIM, (g + 1) * HEAD_DIM)
        ks = slice((g // group) * HEAD_DIM, (g // group + 1) * HEAD_DIM)
        s_ref = s_bufs[u % 2]
        s_ref[...] = lax.dot_general(k_ref[b * ts:(b + 1) * ts, ks], q_ref[:, hs], (((1,), (1,)), ((), ())),
                                     preferred_element_type=f32)
        pmax = None
        for c in range(n_chunks):
            chunk = s_ref[c * SOFTMAX_CHUNK:(c + 1) * SOFTMAX_CHUNK, :]
            cm = jnp.max(chunk.reshape(SOFTMAX_CHUNK // F32_ROWS, F32_ROWS, tq), axis=0)
            pmax = cm if pmax is None else jnp.maximum(pmax, cm)
        return jnp.max(pmax, axis=0, keepdims=True)

    def accumulate(u, m_cur):
        b, g = units[u]
        s_ref, p_ref = s_bufs[u % 2], p_bufs[u % 2]
        m_prev = m_ref[g]
        m_new = jnp.maximum(m_prev, m_cur)
        alpha = jnp.exp2(m_prev - m_new)
        for c in range(n_chunks):
            rows = slice(c * SOFTMAX_CHUNK, (c + 1) * SOFTMAX_CHUNK)
            p_ref[rows, :] = jnp.exp2(s_ref[rows, :] - m_new).astype(bf16)
        vt = vt_ref[(g // group) * HEAD_DIM:(g // group + 1) * HEAD_DIM, b * ts:(b + 1) * ts]
        vt_ext = jnp.concatenate([vt, jnp.ones((BF16_ROWS, ts), bf16)], axis=0)
        acc_ref[g] = alpha * acc_ref[g] + jnp.dot(vt_ext, p_ref[...], preferred_element_type=f32)
        m_ref[g] = m_new

    m_cur = scores(0)
    for u in range(len(units)):
        m_next = scores(u + 1) if u + 1 < len(units) else None
        accumulate(u, m_cur)
        m_cur = m_next

    @pl.when(j == pl.num_programs(1) - 1)
    def _():
        for g in range(heads):
            hs = slice(g * HEAD_DIM, (g + 1) * HEAD_DIM)
            acc = acc_ref[g]
            o_ref[:, hs] = (acc[:HEAD_DIM] / acc[HEAD_DIM:HEAD_DIM + 1]).T.astype(o_ref.dtype)


def _gqa_attention(proj, vt, *, q_col, k_col, q_heads, kv_heads):
    S = proj.shape[0]
    group = q_heads // kv_heads
    qw, kw = q_heads * HEAD_DIM, kv_heads * HEAD_DIM
    vrows = HEAD_DIM + BF16_ROWS
    tq, tk = 512, 4096
    ts = tk // GQA_SUB_BLOCKS
    assert S % tq == 0 and S % tk == 0 and q_col % qw == 0 and k_col % kw == 0 and ts % SOFTMAX_CHUNK == 0
    return pl.pallas_call(
        functools.partial(_gqa_kernel, heads=q_heads, group=group),
        grid=(S // tq, S // tk),
        in_specs=[
            pl.BlockSpec((tq, qw), lambda i, j: (i, q_col // qw)),
            pl.BlockSpec((tk, kw), lambda i, j: (j, k_col // kw)),
            pl.BlockSpec((kw, tk), lambda i, j: (0, j)),
        ],
        out_specs=pl.BlockSpec((tq, qw), lambda i, j: (i, 0)),
        out_shape=jax.ShapeDtypeStruct((S, qw), bf16),
        scratch_shapes=[pltpu.VMEM((q_heads, 1, tq), f32), pltpu.VMEM((q_heads, vrows, tq), f32),
                        pltpu.VMEM((ts, tq), f32), pltpu.VMEM((ts, tq), f32),
                        pltpu.VMEM((ts, tq), bf16), pltpu.VMEM((ts, tq), bf16)],
        compiler_params=_params("parallel", "arbitrary"),
        name="gqa_attn",
    )(proj, proj, vt)


def _out_proj_kernel(na_ref, gq_ref, nag_ref, gqg_ref, w_ref, x_ref, o_ref, wb_ref, *, na_width):
    @pl.when(pl.program_id(0) == 0)
    def _():
        wb_ref[...] = w_ref[...].astype(bf16)

    a = _rms(na_ref[...].astype(f32), nag_ref[...]).astype(bf16)
    b = _rms(gq_ref[...].astype(f32), gqg_ref[...]).astype(bf16)
    o = jnp.dot(a, wb_ref[:na_width, :], preferred_element_type=f32)
    o = o + jnp.dot(b, wb_ref[na_width:, :], preferred_element_type=f32)
    o_ref[...] = x_ref[...] + o


def _out_proj(na_o, gqa_o, na_g, gqa_g, w_out, x2):
    S, D = x2.shape
    na_width, gqa_width = na_o.shape[1], gqa_o.shape[1]
    tm = 512
    assert S % tm == 0
    return pl.pallas_call(
        functools.partial(_out_proj_kernel, na_width=na_width),
        grid=(S // tm,),
        in_specs=[
            pl.BlockSpec((tm, na_width), lambda i: (i, 0)),
            pl.BlockSpec((tm, gqa_width), lambda i: (i, 0)),
            pl.BlockSpec((1, na_width), lambda i: (0, 0)),
            pl.BlockSpec((1, gqa_width), lambda i: (0, 0)),
            pl.BlockSpec((na_width + gqa_width, D), lambda i: (0, 0), pipeline_mode=pl.Buffered(1)),
            pl.BlockSpec((tm, D), lambda i: (i, 0)),
        ],
        out_specs=pl.BlockSpec((tm, D), lambda i: (i, 0)),
        out_shape=jax.ShapeDtypeStruct((S, D), f32),
        scratch_shapes=[pltpu.VMEM((na_width + gqa_width, D), bf16)],
        compiler_params=_params("arbitrary"),
        name="out_proj",
    )(na_o, gqa_o, na_g, gqa_g, w_out, x2)


def _ffn_up_kernel(x_ref, xp_ref, xn_ref, g_ref, wg_ref, wv_ref, cwg_ref, cwv_ref, cbg_ref, cbv_ref,
                   o_ref, h_ref, pg_ref, pv_ref, *, tm):
    i = pl.program_id(0)
    j = pl.program_id(1)
    halo = BF16_ROWS

    @pl.when(j == 0)
    def _():
        g = g_ref[...]
        h_ref[halo:halo + tm, :] = _rms(x_ref[...], g).astype(bf16)
        zeros = jnp.zeros((F32_ROWS, x_ref.shape[1]), f32)
        prev = jnp.where(i > 0, _rms(xp_ref[...], g), 0.0)
        nxt = jnp.where(i < pl.num_programs(0) - 1, _rms(xn_ref[...], g), 0.0)
        h_ref[0:halo, :] = jnp.concatenate([zeros, prev], axis=0).astype(bf16)
        h_ref[halo + tm:, :] = jnp.concatenate([nxt, zeros], axis=0).astype(bf16)

    h = h_ref[...]
    pg_ref[...] = jnp.dot(h, wg_ref[...].astype(bf16), preferred_element_type=f32)
    pv_ref[...] = jnp.dot(h, wv_ref[...].astype(bf16), preferred_element_type=f32)

    def conv(p_ref, cw_ref, cb_ref):
        cw = cw_ref[...]
        p = p_ref[...]
        rows = p.shape[0]
        below = pltpu.roll(p * cw[0:1], 1, 0)
        above = pltpu.roll(p * cw[2:3], rows - 1, 0)
        return (below + p * cw[1:2] + above)[halo:halo + tm] + cb_ref[...]

    gate = conv(pg_ref, cwg_ref, cbg_ref)
    val = conv(pv_ref, cwv_ref, cbv_ref)
    o_ref[...] = (gate / (1.0 + jnp.exp(-gate)) * val).astype(o_ref.dtype)


def _ffn_up(x1, ln_g, w_up, conv_w, conv_b, d_ff):
    S, D = x1.shape
    tm, tn = 1024, 512
    assert S % tm == 0 and d_ff % tn == 0 and tm % F32_ROWS == 0
    nj = d_ff // tn
    rows8 = tm // F32_ROWS
    last8 = S // F32_ROWS - 1
    p_scratch = pltpu.VMEM((tm + 2 * BF16_ROWS, tn), f32)
    n_i = S // tm

    def x_tile(i, j):
        return jnp.minimum(i + (j > 0).astype(jnp.int32), n_i - 1)

    return pl.pallas_call(
        functools.partial(_ffn_up_kernel, tm=tm),
        grid=(n_i, nj),
        in_specs=[
            pl.BlockSpec((tm, D), lambda i, j: (x_tile(i, j), 0)),
            pl.BlockSpec((F32_ROWS, D), lambda i, j: (jnp.maximum(x_tile(i, j) * rows8 - 1, 0), 0)),
            pl.BlockSpec((F32_ROWS, D), lambda i, j: (jnp.minimum((x_tile(i, j) + 1) * rows8, last8), 0)),
            pl.BlockSpec((1, D), lambda i, j: (0, 0)),
            pl.BlockSpec((D, tn), lambda i, j: (0, j)),
            pl.BlockSpec((D, tn), lambda i, j: (0, j + nj)),
            pl.BlockSpec((CONV_W, tn), lambda i, j: (0, j)),
            pl.BlockSpec((CONV_W, tn), lambda i, j: (0, j + nj)),
            pl.BlockSpec((1, tn), lambda i, j: (0, j)),
            pl.BlockSpec((1, tn), lambda i, j: (0, j + nj)),
        ],
        out_specs=pl.BlockSpec((tm, tn), lambda i, j: (i, j)),
        out_shape=jax.ShapeDtypeStruct((S, d_ff), bf16),
        scratch_shapes=[pltpu.VMEM((tm + 2 * BF16_ROWS, D), bf16), p_scratch, p_scratch],
        compiler_params=_params("parallel", "arbitrary"),
        name="ffn_up",
    )(x1, x1, x1, ln_g, w_up, w_up, conv_w, conv_w, conv_b, conv_b)


def _ffn_down_kernel(a_ref, w_ref, x_ref, g_ref, o_ref, *, final_norm):
    y = x_ref[...] + jnp.dot(a_ref[...], w_ref[...], preferred_element_type=f32)
    o_ref[...] = _rms(y, g_ref[...]) if final_norm else y


def _ffn_down(act, w_down, x1, final_g, *, final_norm):
    S, D = x1.shape
    d_ff = act.shape[1]
    tm = 512
    assert S % tm == 0
    return pl.pallas_call(
        functools.partial(_ffn_down_kernel, final_norm=final_norm),
        grid=(S // tm,),
        in_specs=[
            pl.BlockSpec((tm, d_ff), lambda i: (i, 0)),
            pl.BlockSpec((d_ff, D), lambda i: (0, 0), pipeline_mode=pl.Buffered(1)),
            pl.BlockSpec((tm, D), lambda i: (i, 0)),
            pl.BlockSpec((1, D), lambda i: (0, 0)),
        ],
        out_specs=pl.BlockSpec((tm, D), lambda i: (i, 0)),
        out_shape=jax.ShapeDtypeStruct((S, D), f32),
        compiler_params=_params("arbitrary"),
        name="ffn_down",
    )(act, w_down, x1, final_g)


def _rope_tables(S):
    t = np.arange(S)
    row = (t // GRID_W).astype(np.float32)
    col = (t % GRID_W).astype(np.float32)
    axis_dim = HEAD_DIM // 2
    expo = np.arange(0, axis_dim, 2, dtype=np.float32) / np.float32(axis_dim)
    inv_freq = (np.float32(1.0) / np.power(np.float32(ROPE_THETA), expo)).astype(np.float32)
    ang = np.concatenate([row[:, None] * inv_freq[None, :], col[:, None] * inv_freq[None, :]], axis=-1)
    cos, sin = np.cos(ang).astype(np.float32), np.sin(ang).astype(np.float32)
    cos_t = np.repeat(cos, 2, axis=-1)
    sin_t = np.stack([-sin, sin], axis=-1).reshape(S, HEAD_DIM)
    return jnp.asarray(cos_t), jnp.asarray(sin_t)


def _in_proj_columns(q_g, k_g, *, na_width, gqa_width, kv_width, scale):
    q_heads, k_heads = gqa_width // HEAD_DIM, kv_width // HEAD_DIM
    ones = lambda n: jnp.ones((n,), f32)
    zeros = lambda n: jnp.zeros((n,), f32)
    gain = jnp.concatenate([ones(3 * na_width), jnp.tile(q_g.astype(f32), q_heads),
                            jnp.tile(k_g.astype(f32), k_heads), ones(kv_width)])
    rope = jnp.concatenate([zeros(3 * na_width), ones(gqa_width + kv_width), zeros(kv_width)])
    col_scale = jnp.concatenate([jnp.full((na_width,), scale * LOG2E, f32), ones(2 * na_width),
                                 jnp.full((gqa_width,), scale * LOG2E, f32), ones(2 * kv_width)])
    return gain[None], rope[None], col_scale[None]


def kernel(x, ln1_g, w_in, na_rpb, q_norm_g, k_norm_g, na_out_g, gqa_out_g, w_out, ln2_g, w_up,
           conv_w, conv_b, w_down, final_g):
    B, S, D = x.shape
    depth = w_in.shape[0]
    assert B == 1 and S % GRID_W == 0
    na_heads = na_rpb.shape[1]
    na_width = na_heads * HEAD_DIM
    gqa_width = gqa_out_g.shape[1]
    q_heads = gqa_width // HEAD_DIM
    kv_width = (w_in.shape[2] - 3 * na_width - gqa_width) // 2
    kv_heads = kv_width // HEAD_DIM
    d_ff = w_down.shape[1]
    scale = HEAD_DIM ** -0.5
    n_rows = S // GRID_W

    cos_t, sin_t = _rope_tables(S)
    xs = x.reshape(S, D)
    for l in range(depth):
        col_gain, col_rope, col_scale = _in_proj_columns(
            q_norm_g[l], k_norm_g[l], na_width=na_width, gqa_width=gqa_width, kv_width=kv_width, scale=scale)
        proj = _in_proj(xs, ln1_g[l][None], w_in[l], col_gain, col_rope, col_scale, cos_t, sin_t,
                        rope_start=3 * na_width, rope_end=3 * na_width + gqa_width + kv_width)
        bias = _na_bias_tables(na_rpb[l], n_rows)
        na_o = _na_attention(proj, bias, heads=na_heads, na_width=na_width)
        vt = proj[:, 3 * na_width + gqa_width + kv_width:].T
        gqa_o = _gqa_attention(proj, vt, q_col=3 * na_width, k_col=3 * na_width + gqa_width,
                               q_heads=q_heads, kv_heads=kv_heads)
        x1 = _out_proj(na_o, gqa_o, na_out_g[l][None], gqa_out_g[l][None], w_out[l], xs)
        act = _ffn_up(x1, ln2_g[l][None], w_up[l], conv_w[l], conv_b[l][None], d_ff)
        xs = _ffn_down(act, w_down[l].astype(bf16), x1, final_g[None], final_norm=(l == depth - 1))
    return xs.reshape(B, S, D)
```
